```python
import jax
import jax.numpy as jnp
from jax import lax
import numpy as np


D_MODEL = 1024
BATCH = 8
SEQ = 4096
DEPTH = 4

CTX_LEN = 256
GRID_W = 64
N_BRANCH = 4
BRANCH_W = D_MODEL // 2
RET_HEADS = 4
RET_DH = BRANCH_W // RET_HEADS
RET_CHUNK = 128
SC_KERNEL = 3
CF_KERNEL = 31
ATT_HEADS = 8
ATT_KV_HEADS = 2
ATT_DH = BRANCH_W // ATT_HEADS
ATT_KV_W = ATT_KV_HEADS * ATT_DH
ATT_BLOCK = 128
WINDOW = 128
ROPE_BASE = 10000.0
ROPE_AXIS_FREQS = ATT_DH // 4
EPS = 1e-6
NEG_INF = -1e30

IN_NAMES = ('ret_q', 'ret_k', 'ret_v', 'ret_z',
            'sc_b', 'sc_c', 'sc_x', 'sc_z',
            'cf_glu', 'cf_z',
            'att_q', 'att_k', 'att_v', 'att_z',
            'merge_gates')
IN_SIZES = (BRANCH_W, BRANCH_W, BRANCH_W, BRANCH_W,
            BRANCH_W, BRANCH_W, BRANCH_W, BRANCH_W,
            2 * BRANCH_W, BRANCH_W,
            BRANCH_W, ATT_KV_W, ATT_KV_W, BRANCH_W,
            N_BRANCH * D_MODEL)
IN_TOTAL = sum(IN_SIZES)

kernel_name = 'hybrid_parallel_gated_dit_block'


def split_proj(y):
    offsets = [int(o) for o in np.cumsum(IN_SIZES)[:-1]]
    return dict(zip(IN_NAMES, jnp.split(y, offsets, axis=-1)))


def in_cols(w, name):
    i = IN_NAMES.index(name)
    start = sum(IN_SIZES[:i])
    return w[:, start:start + IN_SIZES[i]]


def heads(t, n):
    return t.reshape(t.shape[:-1] + (n, t.shape[-1] // n))


def rmsnorm(x, w):
    xf = x.astype(jnp.float32)
    y = xf * lax.rsqrt(jnp.mean(xf * xf, axis=-1, keepdims=True) + EPS)
    return (y * w).astype(x.dtype)


def layer_norm(u, w, b):
    uf = u.astype(jnp.float32)
    mu = jnp.mean(uf, axis=-1, keepdims=True)
    var = jnp.mean(jnp.square(uf - mu), axis=-1, keepdims=True)
    return (uf - mu) * lax.rsqrt(var + EPS) * w + b


def head_group_norm(o, w):
    of = o.astype(jnp.float32)
    mu = jnp.mean(of, axis=-1, keepdims=True)
    var = jnp.mean(jnp.square(of - mu), axis=-1, keepdims=True)
    y = (of - mu) * lax.rsqrt(var + EPS)
    return y.reshape(o.shape[:-2] + (-1,)) * w


def ada_modulation(cvec, w, b):
    m = jax.nn.silu(cvec) @ w + b
    return jnp.split(m, 3, axis=-1)


def modulate(h, shift, scale):
    return h * (1 + scale) + shift


def depthwise_conv(u, w):
    K, C = w.shape
    return lax.conv_general_dilated(
        u, w[:, None, :].astype(u.dtype), window_strides=(1,), padding=[(K // 2, K // 2)],
        dimension_numbers=('NWC', 'WIO', 'NWC'), feature_group_count=C)


def axial_rope_tables(row, col):
    inv = ROPE_BASE ** (-jnp.arange(ROPE_AXIS_FREQS, dtype=jnp.float32) / ROPE_AXIS_FREQS)
    ang = jnp.concatenate([row.astype(jnp.float32)[:, None] * inv[None],
                           col.astype(jnp.float32)[:, None] * inv[None]], axis=-1)
    return jnp.cos(ang)[:, None, :], jnp.sin(ang)[:, None, :]


def axial_rope(t, cos, sin):
    half = t.shape[-1] // 2
    t1 = t[..., :half].astype(jnp.float32)
    t2 = t[..., half:].astype(jnp.float32)
    return jnp.concatenate([t1 * cos - t2 * sin, t2 * cos + t1 * sin], axis=-1).astype(t.dtype)


def retention_dir(q, k, v, log_gamma, s0, strict):
    B, N, H, d = q.shape
    C = RET_CHUNK
    nc = N // C
    qc, kc, vc = (t.reshape(B, nc, C, H, d) for t in (q, k, v))
    pos = jnp.arange(C)
    diff = pos[:, None] - pos[None, :]
    keep = (diff > 0) if strict else (diff >= 0)
    expo = jnp.where(keep, diff, 0).astype(jnp.float32)
    decay = jnp.where(keep[None], jnp.exp(expo[None] * log_gamma[:, None, None]), 0.0)
    scores = jnp.einsum('bnihd,bnjhd->bnhij', qc, kc) * decay
    intra = jnp.einsum('bnhij,bnjhe->bnihe', scores, vc)
    posf = pos.astype(jnp.float32)
    k_w = jnp.exp((C - 1 - posf)[:, None] * log_gamma[None])
    q_w = jnp.exp((posf + 1)[:, None] * log_gamma[None])
    kv = jnp.einsum('bnjhd,jh,bnjhe->nbhde', kc, k_w, vc)
    chunk_decay = jnp.exp(C * log_gamma)[None, :, None, None]

    def step(s, kv_c):
        return s * chunk_decay + kv_c, s

    s_final, s_prev = lax.scan(step, s0, kv)
    cross = jnp.einsum('bnihd,ih,nbhde->bnihe', qc, q_w, s_prev)
    return (intra + cross).reshape(B, N, H, d), s_final


def retention_bidir(q, k, v, log_gamma, s0_f, s0_b):
    out_f, s_f = retention_dir(q, k, v, log_gamma[0], s0_f, False)
    flip = lambda t: jnp.flip(t, axis=1)
    out_b, s_b = retention_dir(flip(q), flip(k), flip(v), log_gamma[1], s0_b, True)
    return out_f + flip(out_b), s_f, s_b


def retention_context_states(k, v, log_gamma):
    L = k.shape[1]
    pos = jnp.arange(L, dtype=jnp.float32)
    w_f = jnp.exp((L - 1 - pos)[:, None] * log_gamma[0][None])
    w_b = jnp.exp(pos[:, None] * log_gamma[1][None])
    s_f = jnp.einsum('blhd,lh,blhe->bhde', k, w_f, v)
    s_b = jnp.einsum('blhd,lh,blhe->bhde', k, w_b, v)
    return s_f, s_b


def windowed_gqa(q, k, v, k_ctx, v_ctx, sink):
    B, S, H, d = q.shape
    G = H // ATT_KV_HEADS
    nb = S // ATT_BLOCK
    L = k_ctx.shape[1]
    qb = q.reshape(B, nb, ATT_BLOCK, ATT_KV_HEADS, G, d)

    def band(t):
        tp = jnp.pad(t, ((0, 0), (ATT_BLOCK, ATT_BLOCK), (0, 0), (0, 0)))
        tp = tp.reshape(B, nb + 2, ATT_BLOCK, ATT_KV_HEADS, d)
        return jnp.concatenate([tp[:, :-2], tp[:, 1:-1], tp[:, 2:]], axis=2)

    kw, vw = band(k), band(v)
    qi = jnp.arange(ATT_BLOCK)
    kj = jnp.arange(3 * ATT_BLOCK) - ATT_BLOCK
    in_band = jnp.abs(kj[None, :] - qi[:, None]) <= WINDOW
    key_pos = jnp.arange(nb)[:, None] * ATT_BLOCK + kj[None, :]
    in_seq = (key_pos >= 0) & (key_pos < S)
    mask = in_band[None] & in_seq[:, None, :]
    scale = d ** -0.5
    s_loc = jnp.einsum('bnqhgd,bnkhd->bnhgqk', qb, kw).astype(jnp.float32) * scale
    s_loc = jnp.where(mask[None, :, None, None], s_loc, NEG_INF)
    s_ctx = jnp.einsum('bnqhgd,bchd->bnhgqc', qb, k_ctx).astype(jnp.float32) * scale
    s_sink = jnp.broadcast_to(sink.astype(jnp.float32).reshape(ATT_KV_HEADS, G, 1, 1),
                              s_loc.shape[:-1] + (1,))
    p = jax.nn.softmax(jnp.concatenate([s_loc, s_ctx, s_sink], axis=-1), axis=-1)
    nk = 3 * ATT_BLOCK
    p_loc = p[..., :nk].astype(v.dtype)
    p_ctx = p[..., nk:nk + L].astype(v.dtype)
    o = (jnp.einsum('bnhgqk,bnkhd->bnqhgd', p_loc, vw)
         + jnp.einsum('bnhgqc,bchd->bnqhgd', p_ctx, v_ctx))
    return o.reshape(B, S, H * d)


def context_attention(q, k, v, sink):
    B, L, H, d = q.shape
    G = H // ATT_KV_HEADS
    qg = q.reshape(B, L, ATT_KV_HEADS, G, d)
    s = jnp.einsum('bqhgd,bkhd->bhgqk', qg, k).astype(jnp.float32) * d ** -0.5
    s_sink = jnp.broadcast_to(sink.astype(jnp.float32).reshape(ATT_KV_HEADS, G, 1, 1), s.shape[:-1] + (1,))
    p = jax.nn.softmax(jnp.concatenate([s, s_sink], axis=-1), axis=-1)[..., :L]
    o = jnp.einsum('bhgqk,bkhd->bqhgd', p.astype(v.dtype), v)
    return o.reshape(B, L, H * d)


def short_conv(b_gate, c_gate, xv, w):
    return b_gate * depthwise_conv(c_gate * xv, w)


def conformer_conv(glu_in, w, bias, ln_w, ln_b):
    a, g = jnp.split(glu_in, 2, axis=-1)
    u = depthwise_conv(a * jax.nn.sigmoid(g), w) + bias
    return jax.nn.silu(layer_norm(u, ln_w, ln_b)).astype(glu_in.dtype)


def mixer_output(p, ret_o, att_o, ret_gn_w, sc_conv_w, cf_conv_w, cf_conv_b, cf_ln_w, cf_ln_b,
                 w_branch, w_out):
    dt = p['ret_z'].dtype
    r = (head_group_norm(ret_o, ret_gn_w) * jax.nn.silu(p['ret_z'])).astype(dt)
    s = short_conv(p['sc_b'], p['sc_c'], p['sc_x'], sc_conv_w) * jax.nn.silu(p['sc_z'])
    f = conformer_conv(p['cf_glu'], cf_conv_w, cf_conv_b, cf_ln_w, cf_ln_b) * jax.nn.silu(p['cf_z'])
    a = att_o * jax.nn.silu(p['att_z'])
    gates = jnp.split(p['merge_gates'], N_BRANCH, axis=-1)
    merged = sum(jax.nn.sigmoid(g) * (br @ w_branch[i])
                 for i, (g, br) in enumerate(zip(gates, (r, s, f, a))))
    return merged @ w_out


def setup_inputs(seed: int = 0) -> dict:
    key = jax.random.key(seed)
    ks = jax.random.split(key, 20)
    f32 = jnp.float32
    D = D_MODEL

    def nrm(k, shape, fan_in):
        return jax.random.normal(k, shape, f32) * fan_in ** -0.5

    decay_logit = jnp.log(2.0 ** (5.0 + jnp.arange(RET_HEADS, dtype=f32)) - 1.0)
    return {
        'x': jax.random.normal(ks[0], (BATCH, SEQ, D), f32),
        'c': jax.random.normal(ks[1], (BATCH, D), f32),
        'ctx': jax.random.normal(ks[2], (BATCH, CTX_LEN, D), f32),
        'c_ctx': jax.random.normal(ks[3], (D,), f32),
        'w_mod': nrm(ks[4], (DEPTH, D, 3 * D), D),
        'b_mod': 0.02 * jax.random.normal(ks[5], (DEPTH, 3 * D), f32),
        'norm_w': 1.0 + 0.1 * jax.random.normal(ks[6], (DEPTH, D), f32),
        'w_in': nrm(ks[7], (DEPTH, D, IN_TOTAL), D),
        'ret_decay': decay_logit[None, None, :] + 0.1 * jax.random.normal(ks[8], (DEPTH, 2, RET_HEADS), f32),
        'ret_gn_w': 1.0 + 0.1 * jax.random.normal(ks[9], (DEPTH, BRANCH_W), f32),
        'sc_conv_w': nrm(ks[10], (DEPTH, SC_KERNEL, BRANCH_W), SC_KERNEL),
        'cf_conv_w': nrm(ks[11], (DEPTH, CF_KERNEL, BRANCH_W), CF_KERNEL),
        'cf_conv_b': 0.02 * jax.random.normal(ks[12], (DEPTH, BRANCH_W), f32),
        'cf_ln_w': 1.0 + 0.1 * jax.random.normal(ks[13], (DEPTH, BRANCH_W), f32),
        'cf_ln_b': 0.02 * jax.random.normal(ks[14], (DEPTH, BRANCH_W), f32),
        'att_sink': jax.random.normal(ks[15], (DEPTH, ATT_HEADS), f32),
        'w_branch': nrm(ks[16], (DEPTH, N_BRANCH, BRANCH_W, D), BRANCH_W),
        'w_out': nrm(ks[17], (DEPTH, D, D), D),
        'final_norm_w': 1.0 + 0.1 * jax.random.normal(ks[18], (D,), f32),
    }


def reference(x, c, ctx, c_ctx, w_mod, b_mod, norm_w, w_in, ret_decay, ret_gn_w, sc_conv_w,
              cf_conv_w, cf_conv_b, cf_ln_w, cf_ln_b, att_sink, w_branch, w_out, final_norm_w):
    B, S, _ = x.shape
    ROWS = S // GRID_W
    row = jnp.repeat(jnp.arange(ROWS), GRID_W)
    col = jnp.tile(jnp.arange(GRID_W), ROWS)
    cos, sin = axial_rope_tables(row, col)
    k_scale = RET_DH ** -0.5
    xc = ctx
    for l in range(DEPTH):
        ctx_out_needed = l < DEPTH - 1
        log_gamma = jax.nn.log_sigmoid(ret_decay[l].astype(jnp.float32))
        shift_x, scale_x, gate_x = ada_modulation(c, w_mod[l], b_mod[l])
        shift_c, scale_c, gate_c = ada_modulation(c_ctx, w_mod[l], b_mod[l])
        hx = modulate(rmsnorm(x, norm_w[l]), shift_x[:, None], scale_x[:, None])
        hc = modulate(rmsnorm(xc, norm_w[l]), shift_c, scale_c)
        px = split_proj(hx @ w_in[l])
        if ctx_out_needed:
            pc = split_proj(hc @ w_in[l])
        else:
            pc = {n: hc @ in_cols(w_in[l], n) for n in ('ret_k', 'ret_v', 'att_k', 'att_v')}
        rk_c = heads(pc['ret_k'], RET_HEADS) * k_scale
        rv_c = heads(pc['ret_v'], RET_HEADS)
        ak_c = heads(pc['att_k'], ATT_KV_HEADS)
        av_c = heads(pc['att_v'], ATT_KV_HEADS)
        if ctx_out_needed:
            zero = jnp.zeros((B, RET_HEADS, RET_DH, RET_DH), jnp.float32)
            ret_c, s_f, s_b = retention_bidir(heads(pc['ret_q'], RET_HEADS), rk_c, rv_c, log_gamma, zero, zero)
            att_c = context_attention(heads(pc['att_q'], ATT_HEADS), ak_c, av_c, att_sink[l])
            out_c = mixer_output(pc, ret_c, att_c, ret_gn_w[l], sc_conv_w[l], cf_conv_w[l], cf_conv_b[l],
                                 cf_ln_w[l], cf_ln_b[l], w_branch[l], w_out[l])
        else:
            s_f, s_b = retention_context_states(rk_c, rv_c, log_gamma)
        ret_x, _, _ = retention_bidir(heads(px['ret_q'], RET_HEADS), heads(px['ret_k'], RET_HEADS) * k_scale,
                                      heads(px['ret_v'], RET_HEADS), log_gamma, s_f, s_b)
        att_x = windowed_gqa(axial_rope(heads(px['att_q'], ATT_HEADS), cos, sin),
                             axial_rope(heads(px['att_k'], ATT_KV_HEADS), cos, sin),
                             heads(px['att_v'], ATT_KV_HEADS), ak_c, av_c, att_sink[l])
        out_x = mixer_output(px, ret_x, att_x, ret_gn_w[l], sc_conv_w[l], cf_conv_w[l], cf_conv_b[l],
                             cf_ln_w[l], cf_ln_b[l], w_branch[l], w_out[l])
        x = x + gate_x[:, None] * out_x
        if ctx_out_needed:
            xc = xc + gate_c * out_c
    return rmsnorm(x, final_norm_w)
```

```python
import functools

import numpy as np
import jax
import jax.numpy as jnp
from jax import lax
from jax.experimental import pallas as pl
from jax.experimental.pallas import tpu as pltpu

D_MODEL = 1024
DEPTH = 4
GRID_W = 64
BRANCH_W = D_MODEL // 2
RET_HEADS = 4
RET_DH = BRANCH_W // RET_HEADS
RET_CHUNK = 128
SC_KERNEL = 3
CF_KERNEL = 31
ATT_HEADS = 8
ATT_KV_HEADS = 2
ATT_GROUP = ATT_HEADS // ATT_KV_HEADS
ATT_DH = BRANCH_W // ATT_HEADS
ATT_KV_W = ATT_KV_HEADS * ATT_DH
ATT_BLOCK = 128
WINDOW = 128
ROPE_BASE = 10000.0
ROPE_AXIS_FREQS = ATT_DH // 4
EPS = 1e-6
NEG_INF = -1e30

RET_COLS = (0, 4 * BRANCH_W)
CONV_COLS = (4 * BRANCH_W, 11 * BRANCH_W)
ATT_COLS = (11 * BRANCH_W, 13 * BRANCH_W + 2 * ATT_KV_W)
GATE_COLS = (13 * BRANCH_W + 2 * ATT_KV_W, 13 * BRANCH_W + 2 * ATT_KV_W + 4 * D_MODEL)

LANES = 128
CONV_HALO = 16
TILE_X = 512
VMEM_LIMIT = 56 * 1024 * 1024

F32 = jnp.float32
BF16 = jnp.bfloat16


def _dot(a, b):
    return jnp.dot(a, b, preferred_element_type=F32)


def _dot_nt(a, b):
    return lax.dot_general(a, b, (((1,), (1,)), ((), ())), preferred_element_type=F32)


def _dot_tn(a, b):
    return lax.dot_general(a, b, (((0,), (0,)), ((), ())), preferred_element_type=F32)


def _sigmoid(v):
    return jax.nn.sigmoid(v)


def _silu(v):
    return v * _sigmoid(v)


def _normed(x, mod_ref, nw_ref):
    a = nw_ref[...] * (1.0 + mod_ref[0, 1:2, :])
    shift = mod_ref[0, 0:1, :]
    ms = jnp.mean(x * x, axis=-1, keepdims=True)
    return (x * lax.rsqrt(ms + EPS) * a + shift).astype(BF16)


def _params():
    return pltpu.CompilerParams(vmem_limit_bytes=VMEM_LIMIT)


def _mod_kernel(c_ref, w_ref, b_ref, o_ref):
    cv = c_ref[...]
    o_ref[0] = _dot(_silu(cv).astype(BF16), w_ref[0].astype(BF16)) + b_ref[0]


def _modulation(cc, w_mod, b_mod):
    rows = cc.shape[0]
    tn = 1024
    return pl.pallas_call(
        _mod_kernel,
        grid=(DEPTH, 3 * D_MODEL // tn),
        in_specs=[pl.BlockSpec((rows, D_MODEL), lambda l, n: (0, 0)),
                  pl.BlockSpec((1, D_MODEL, tn), lambda l, n: (l, 0, n)),
                  pl.BlockSpec((1, 1, tn), lambda l, n: (l, 0, n))],
        out_specs=pl.BlockSpec((1, rows, tn), lambda l, n: (l, 0, n)),
        out_shape=jax.ShapeDtypeStruct((DEPTH, rows, 3 * D_MODEL), F32),
        compiler_params=_params(),
        name="modulation",
    )(cc, w_mod, b_mod.reshape(DEPTH, 1, 3 * D_MODEL))


_T_DEC, _T_QF, _T_KF, _T_QB, _T_KB, _T_CF, _T_CB = range(7)


def _ret_kernel(T, S, x_ref, mod_ref, nw_ref, w_ref, dec_ref, gn_ref, s0f_ref, s0b_ref,
                r_ref, sf_ref, sb_ref, qkv, zbuf, obuf, st_f, st_b, tab):
    p = pl.program_id(1)
    t = pl.program_id(2)
    nT = S // T
    C = RET_CHUNK
    H = RET_HEADS
    k_scale = RET_DH ** -0.5

    @pl.when((p == 0) & (t == 0))
    def _init():
        raw = dec_ref[...]
        lg = jnp.minimum(raw, 0.0) - jnp.log(1.0 + jnp.exp(-jnp.abs(raw)))
        ii = lax.broadcasted_iota(jnp.int32, (C, C), 0)
        jj = lax.broadcasted_iota(jnp.int32, (C, C), 1)
        diff = (ii - jj).astype(F32)
        pos = ii.astype(F32)
        for h in range(H):
            lf = lg[h:h + 1, :]
            lb = lg[H + h:H + h + 1, :]
            dec_f = jnp.where(diff >= 0, jnp.exp(jnp.where(diff >= 0, diff, 0.0) * lf), 0.0)
            dec_b = jnp.where(diff < 0, jnp.exp(jnp.where(diff < 0, -diff, 0.0) * lb), 0.0)
            tab[_T_DEC * H + h] = (dec_f + dec_b) * k_scale
            tab[_T_QF * H + h] = jnp.exp((pos + 1.0) * lf)
            tab[_T_KF * H + h] = jnp.exp((C - 1.0 - pos) * lf) * k_scale
            tab[_T_QB * H + h] = jnp.exp((C - pos) * lb)
            tab[_T_KB * H + h] = jnp.exp(pos * lb) * k_scale
            tab[_T_CF * H + h] = jnp.exp(float(C) * lf) + jnp.zeros((C, LANES), F32)
            tab[_T_CB * H + h] = jnp.exp(float(C) * lb) + jnp.zeros((C, LANES), F32)
        st_f[...] = s0f_ref[0]
        st_b[...] = s0b_ref[0]

    def head_cols(part, h):
        return slice(part * BRANCH_W + h * RET_DH, part * BRANCH_W + (h + 1) * RET_DH)

    @pl.when(p == 0)
    def _forward():
        h_tile = _normed(x_ref[0], mod_ref, nw_ref)
        proj = _dot(h_tile, w_ref[0])
        row0 = pl.multiple_of(t * T, T)
        qkv[pl.ds(row0, T), :] = proj[:, :3 * BRANCH_W].astype(BF16)
        zbuf[pl.ds(row0, T), :] = proj[:, 3 * BRANCH_W:]
        for c in range(T // C):
            rows = pl.ds(pl.multiple_of(t * T + c * C, C), C)
            for h in range(H):
                q = qkv[rows, head_cols(0, h)]
                k = qkv[rows, head_cols(1, h)]
                v = qkv[rows, head_cols(2, h)]
                sc = _dot_nt(q, k) * tab[_T_DEC * H + h]
                intra = _dot(sc.astype(BF16), v)
                cross = _dot(q, st_f[h].astype(BF16)) * tab[_T_QF * H + h]
                obuf[rows, h * RET_DH:(h + 1) * RET_DH] = intra + cross
                kw = (k.astype(F32) * tab[_T_KF * H + h]).astype(BF16)
                st_f[h] = st_f[h] * tab[_T_CF * H + h] + _dot_tn(kw, v)

        @pl.when(t == nT - 1)
        def _():
            sf_ref[0] = st_f[...]

    @pl.when(p == 1)
    def _backward():
        tt = nT - 1 - t
        for c in reversed(range(T // C)):
            rows = pl.ds(pl.multiple_of(tt * T + c * C, C), C)
            for h in range(H):
                hc = slice(h * RET_DH, (h + 1) * RET_DH)
                q = qkv[rows, head_cols(0, h)]
                k = qkv[rows, head_cols(1, h)]
                v = qkv[rows, head_cols(2, h)]
                cross = _dot(q, st_b[h].astype(BF16)) * tab[_T_QB * H + h]
                o = obuf[rows, hc] + cross
                mu = jnp.mean(o, axis=-1, keepdims=True)
                d = o - mu
                var = jnp.mean(d * d, axis=-1, keepdims=True)
                y = d * lax.rsqrt(var + EPS) * gn_ref[:, hc]
                r_ref[0, c * C:(c + 1) * C, hc] = (y * _silu(zbuf[rows, hc])).astype(BF16)
                kw = (k.astype(F32) * tab[_T_KB * H + h]).astype(BF16)
                st_b[h] = st_b[h] * tab[_T_CB * H + h] + _dot_tn(kw, v)

        @pl.when(t == nT - 1)
        def _():
            sb_ref[0] = st_b[...]


def _retention(xs, mod, nw, w_ret, l, dec, gn_w, s0f, s0b, T):
    B, S, _ = xs.shape
    nT = S // T
    st_shape = (RET_HEADS, RET_DH, RET_DH)
    wcols = RET_COLS[1] - RET_COLS[0]
    return pl.pallas_call(
        functools.partial(_ret_kernel, T, S),
        grid=(B, 2, nT),
        in_specs=[
            pl.BlockSpec((1, T, D_MODEL), lambda b, p, t: (b, t * (1 - p) + (nT - 1) * p, 0)),
            pl.BlockSpec((1, 3, D_MODEL), lambda b, p, t: (b, 0, 0)),
            pl.BlockSpec((1, D_MODEL), lambda b, p, t: (0, 0)),
            pl.BlockSpec((1, D_MODEL, wcols), lambda b, p, t: (l, 0, 0)),
            pl.BlockSpec((2 * RET_HEADS, LANES), lambda b, p, t: (0, 0)),
            pl.BlockSpec((1, BRANCH_W), lambda b, p, t: (0, 0)),
            pl.BlockSpec((1,) + st_shape, lambda b, p, t: (b, 0, 0, 0)),
            pl.BlockSpec((1,) + st_shape, lambda b, p, t: (b, 0, 0, 0)),
        ],
        out_specs=[
            pl.BlockSpec((1, T, BRANCH_W), lambda b, p, t: (b, nT - 1 - t * p, 0)),
            pl.BlockSpec((1,) + st_shape, lambda b, p, t: (b, 0, 0, 0)),
            pl.BlockSpec((1,) + st_shape, lambda b, p, t: (b, 0, 0, 0)),
        ],
        out_shape=[
            jax.ShapeDtypeStruct((B, S, BRANCH_W), BF16),
            jax.ShapeDtypeStruct((B,) + st_shape, F32),
            jax.ShapeDtypeStruct((B,) + st_shape, F32),
        ],
        scratch_shapes=[
            pltpu.VMEM((S, 3 * BRANCH_W), BF16),
            pltpu.VMEM((S, BRANCH_W), F32),
            pltpu.VMEM((S, BRANCH_W), F32),
            pltpu.VMEM(st_shape, F32),
            pltpu.VMEM(st_shape, F32),
            pltpu.VMEM((7 * RET_HEADS, RET_CHUNK, LANES), F32),
        ],
        compiler_params=_params(),
        name="retention",
    )(xs, mod, nw, w_ret, dec, gn_w, s0f, s0b)


def _rope(val, rope_ref, r0, nrows):
    cos = rope_ref[0, pl.ds(r0, nrows), :]
    sin_lo = rope_ref[1, pl.ds(r0, nrows), :]
    sin_hi = rope_ref[2, pl.ds(r0, nrows), :]
    outs = []
    for g in range(val.shape[1] // LANES):
        vg = val[:, g * LANES:(g + 1) * LANES]
        outs.append(vg * cos + pltpu.roll(vg, LANES - ATT_DH // 2, 1) * sin_lo
                    + pltpu.roll(vg, ATT_DH // 2, 1) * sin_hi)
    return outs[0] if len(outs) == 1 else jnp.concatenate(outs, axis=1)


def _softmax_av(qh, parts, sink):
    scores = []
    m = None
    for k, _, mask in parts:
        s = _dot_nt(qh, k)
        if mask is not None:
            s = jnp.where(mask, s, NEG_INF)
        scores.append(s)
        sm = jnp.max(s, axis=-1, keepdims=True)
        m = sm if m is None else jnp.maximum(m, sm)
    m = jnp.maximum(m, sink)
    den = jnp.exp(sink - m)
    o = None
    for s, (_, v, _) in zip(scores, parts):
        pr = jnp.exp(s - m)
        den = den + jnp.sum(pr, axis=-1, keepdims=True)
        ov = _dot(pr.astype(BF16), v)
        o = ov if o is None else o + ov
    return o / den


def _att_local_kernel(T, S, x_ref, xp_ref, xn_ref, mod_ref, nw_ref, w_ref, rope_ref, kc_ref, vc_ref,
                      sink_ref, a_ref, q_s, k_s, v_s, z_s):
    t = pl.program_id(1)
    nb = T // ATT_BLOCK
    blk = ATT_BLOCK
    q_cols = slice(0, BRANCH_W)
    kv_cols = slice(BRANCH_W, BRANCH_W + 2 * ATT_KV_W)
    z_cols = slice(BRANCH_W + 2 * ATT_KV_W, 2 * BRANCH_W + 2 * ATT_KV_W)

    hm = _normed(x_ref[0], mod_ref, nw_ref)
    row0 = pl.multiple_of(t * T, T)
    q = _rope(_dot(hm, w_ref[0, :, q_cols]), rope_ref, row0, T) * (ATT_DH ** -0.5)
    q_s[...] = q.astype(BF16)
    kv = _dot(hm, w_ref[0, :, kv_cols])
    k_s[blk:blk + T, :] = _rope(kv[:, :ATT_KV_W], rope_ref, row0, T).astype(BF16)
    v_s[blk:blk + T, :] = kv[:, ATT_KV_W:].astype(BF16)
    z_s[...] = _silu(_dot(hm, w_ref[0, :, z_cols]))

    rp = pl.multiple_of(jnp.maximum(t * T - blk, 0), blk)
    kvp = _dot(_normed(xp_ref[0], mod_ref, nw_ref), w_ref[0, :, kv_cols])
    k_s[0:blk, :] = _rope(kvp[:, :ATT_KV_W], rope_ref, rp, blk).astype(BF16)
    v_s[0:blk, :] = kvp[:, ATT_KV_W:].astype(BF16)
    rn = pl.multiple_of(jnp.minimum((t + 1) * T, S - blk), blk)
    kvn = _dot(_normed(xn_ref[0], mod_ref, nw_ref), w_ref[0, :, kv_cols])
    k_s[blk + T:2 * blk + T, :] = _rope(kvn[:, :ATT_KV_W], rope_ref, rn, blk).astype(BF16)
    v_s[blk + T:2 * blk + T, :] = kvn[:, ATT_KV_W:].astype(BF16)

    qi = lax.broadcasted_iota(jnp.int32, (blk, 3 * blk), 0)
    kj = lax.broadcasted_iota(jnp.int32, (blk, 3 * blk), 1)
    band = jnp.abs(kj - blk - qi) <= WINDOW
    for j in range(nb):
        gb = t * nb + j
        mask = band & ((kj >= blk) | (gb > 0)) & ((kj < 2 * blk) | (gb < S // blk - 1))
        qrows = slice(j * blk, (j + 1) * blk)
        wrows = slice(j * blk, (j + 3) * blk)
        for hk in range(ATT_KV_HEADS):
            kvc = slice(hk * ATT_DH, (hk + 1) * ATT_DH)
            parts = [(k_s[wrows, kvc], v_s[wrows, kvc], mask),
                     (kc_ref[0, :, kvc], vc_ref[0, :, kvc], None)]
            for g in range(ATT_GROUP):
                hd = hk * ATT_GROUP + g
                hc = slice(hd * ATT_DH, (hd + 1) * ATT_DH)
                o = _softmax_av(q_s[qrows, hc], parts, sink_ref[hd])
                a_ref[0, qrows, hc] = (o * z_s[qrows, hc]).astype(BF16)


def _att_ctx_kernel(x_ref, mod_ref, nw_ref, w_ref, sink_ref, a_ref, kc_ref, vc_ref, q_s, z_s):
    q_cols = slice(0, BRANCH_W)
    kv_cols = slice(BRANCH_W, BRANCH_W + 2 * ATT_KV_W)
    z_cols = slice(BRANCH_W + 2 * ATT_KV_W, 2 * BRANCH_W + 2 * ATT_KV_W)
    hm = _normed(x_ref[0], mod_ref, nw_ref)
    q_s[...] = (_dot(hm, w_ref[0, :, q_cols]) * (ATT_DH ** -0.5)).astype(BF16)
    kv = _dot(hm, w_ref[0, :, kv_cols])
    kc_ref[0] = kv[:, :ATT_KV_W].astype(BF16)
    vc_ref[0] = kv[:, ATT_KV_W:].astype(BF16)
    z_s[...] = _silu(_dot(hm, w_ref[0, :, z_cols]))
    for hk in range(ATT_KV_HEADS):
        kvc = slice(hk * ATT_DH, (hk + 1) * ATT_DH)
        parts = [(kc_ref[0, :, kvc], vc_ref[0, :, kvc], None)]
        for g in range(ATT_GROUP):
            hd = hk * ATT_GROUP + g
            hc = slice(hd * ATT_DH, (hd + 1) * ATT_DH)
            o = _softmax_av(q_s[:, hc], parts, sink_ref[hd])
            a_ref[0, :, hc] = (o * z_s[:, hc]).astype(BF16)


def _attention_local(xs, mod, nw, w_att, l, rope_tab, kc, vc, sink, T):
    B, S, _ = xs.shape
    nT = S // T
    bpt = T // ATT_BLOCK
    nblk = S // ATT_BLOCK
    L = kc.shape[1]
    wcols = ATT_COLS[1] - ATT_COLS[0]
    return pl.pallas_call(
        functools.partial(_att_local_kernel, T, S),
        grid=(B, nT),
        in_specs=[
            pl.BlockSpec((1, T, D_MODEL), lambda b, t: (b, t, 0)),
            pl.BlockSpec((1, ATT_BLOCK, D_MODEL), lambda b, t: (b, jnp.maximum(t * bpt - 1, 0), 0)),
            pl.BlockSpec((1, ATT_BLOCK, D_MODEL), lambda b, t: (b, jnp.minimum((t + 1) * bpt, nblk - 1), 0)),
            pl.BlockSpec((1, 3, D_MODEL), lambda b, t: (b, 0, 0)),
            pl.BlockSpec((1, D_MODEL), lambda b, t: (0, 0)),
            pl.BlockSpec((1, D_MODEL, wcols), lambda b, t: (l, 0, 0)),
            pl.BlockSpec((3, S, LANES), lambda b, t: (0, 0, 0)),
            pl.BlockSpec((1, L, ATT_KV_W), lambda b, t: (b, 0, 0)),
            pl.BlockSpec((1, L, ATT_KV_W), lambda b, t: (b, 0, 0)),
            pl.BlockSpec(memory_space=pltpu.SMEM),
        ],
        out_specs=pl.BlockSpec((1, T, BRANCH_W), lambda b, t: (b, t, 0)),
        out_shape=jax.ShapeDtypeStruct((B, S, BRANCH_W), BF16),
        scratch_shapes=[
            pltpu.VMEM((T, BRANCH_W), BF16),
            pltpu.VMEM((T + 2 * ATT_BLOCK, ATT_KV_W), BF16),
            pltpu.VMEM((T + 2 * ATT_BLOCK, ATT_KV_W), BF16),
            pltpu.VMEM((T, BRANCH_W), F32),
        ],
        compiler_params=_params(),
        name="attention_local",
    )(xs, xs, xs, mod, nw, w_att, rope_tab, kc, vc, sink)


def _attention_ctx(xc, mod, nw, w_att, l, sink):
    B, L, _ = xc.shape
    wcols = ATT_COLS[1] - ATT_COLS[0]
    return pl.pallas_call(
        _att_ctx_kernel,
        grid=(B,),
        in_specs=[
            pl.BlockSpec((1, L, D_MODEL), lambda b: (b, 0, 0)),
            pl.BlockSpec((1, 3, D_MODEL), lambda b: (b, 0, 0)),
            pl.BlockSpec((1, D_MODEL), lambda b: (0, 0)),
            pl.BlockSpec((1, D_MODEL, wcols), lambda b: (l, 0, 0)),
            pl.BlockSpec(memory_space=pltpu.SMEM),
        ],
        out_specs=[
            pl.BlockSpec((1, L, BRANCH_W), lambda b: (b, 0, 0)),
            pl.BlockSpec((1, L, ATT_KV_W), lambda b: (b, 0, 0)),
            pl.BlockSpec((1, L, ATT_KV_W), lambda b: (b, 0, 0)),
        ],
        out_shape=[
            jax.ShapeDtypeStruct((B, L, BRANCH_W), BF16),
            jax.ShapeDtypeStruct((B, L, ATT_KV_W), BF16),
            jax.ShapeDtypeStruct((B, L, ATT_KV_W), BF16),
        ],
        scratch_shapes=[
            pltpu.VMEM((L, BRANCH_W), BF16),
            pltpu.VMEM((L, BRANCH_W), F32),
        ],
        compiler_params=_params(),
        name="attention_ctx",
    )(xc, mod, nw, w_att, sink)


def _conv_kernel(T, S, x_ref, xp_ref, xn_ref, mod_ref, nw_ref, w_ref, scw_ref, cfw_ref, cfb_ref,
                 lnw_ref, lnb_ref, s_ref, f_ref, hbuf, u_sc, u_cf):
    t = pl.program_id(1)
    nT = S // T
    HL = CONV_HALO
    W = BRANCH_W
    hbuf[0:HL, :] = _normed(xp_ref[0], mod_ref, nw_ref)
    hbuf[HL:HL + T, :] = _normed(x_ref[0], mod_ref, nw_ref)
    hbuf[HL + T:2 * HL + T, :] = _normed(xn_ref[0], mod_ref, nw_ref)
    hext = hbuf[...]
    hm = hbuf[HL:HL + T, :]

    cx = _dot(hext, w_ref[0, :, W:3 * W])
    u_sc[...] = cx[:, :W] * cx[:, W:]
    glu = _dot(hext, w_ref[0, :, 4 * W:6 * W])
    u_cf[...] = glu[:, :W] * _sigmoid(glu[:, W:])

    @pl.when(t == 0)
    def _():
        u_sc[0:HL, :] = jnp.zeros((HL, W), F32)
        u_cf[0:HL, :] = jnp.zeros((HL, W), F32)

    @pl.when(t == nT - 1)
    def _():
        u_sc[HL + T:2 * HL + T, :] = jnp.zeros((HL, W), F32)
        u_cf[HL + T:2 * HL + T, :] = jnp.zeros((HL, W), F32)

    b_gate = _dot(hm, w_ref[0, :, 0:W])
    z_sc = _dot(hm, w_ref[0, :, 3 * W:4 * W])
    z_cf = _dot(hm, w_ref[0, :, 6 * W:7 * W])

    RB = 64
    for rb in range(T // RB):
        r = rb * RB
        acc = scw_ref[0:1, :] * u_sc[HL - 1 + r:HL - 1 + r + RB, :]
        for k in range(1, SC_KERNEL):
            o = HL - SC_KERNEL // 2 + k + r
            acc = acc + scw_ref[k:k + 1, :] * u_sc[o:o + RB, :]
        s_ref[0, r:r + RB, :] = (b_gate[r:r + RB] * acc * _silu(z_sc[r:r + RB])).astype(BF16)

        acc = cfb_ref[...] + cfw_ref[0:1, :] * u_cf[HL - CF_KERNEL // 2 + r:HL - CF_KERNEL // 2 + r + RB, :]
        for k in range(1, CF_KERNEL):
            o = HL - CF_KERNEL // 2 + k + r
            acc = acc + cfw_ref[k:k + 1, :] * u_cf[o:o + RB, :]
        mu = jnp.mean(acc, axis=-1, keepdims=True)
        d = acc - mu
        var = jnp.mean(d * d, axis=-1, keepdims=True)
        y = d * lax.rsqrt(var + EPS) * lnw_ref[...] + lnb_ref[...]
        f_ref[0, r:r + RB, :] = (_silu(y) * _silu(z_cf[r:r + RB])).astype(BF16)


def _conv_branches(xs, mod, nw, w_conv, l, scw, cfw, cfb, lnw, lnb, T):
    B, S, _ = xs.shape
    nT = S // T
    HL = CONV_HALO
    bpt = T // HL
    nblk = S // HL
    wcols = CONV_COLS[1] - CONV_COLS[0]
    vec = lambda: pl.BlockSpec((1, BRANCH_W), lambda b, t: (0, 0))
    return pl.pallas_call(
        functools.partial(_conv_kernel, T, S),
        grid=(B, nT),
        in_specs=[
            pl.BlockSpec((1, T, D_MODEL), lambda b, t: (b, t, 0)),
            pl.BlockSpec((1, HL, D_MODEL), lambda b, t: (b, jnp.maximum(t * bpt - 1, 0), 0)),
            pl.BlockSpec((1, HL, D_MODEL), lambda b, t: (b, jnp.minimum((t + 1) * bpt, nblk - 1), 0)),
            pl.BlockSpec((1, 3, D_MODEL), lambda b, t: (b, 0, 0)),
            pl.BlockSpec((1, D_MODEL), lambda b, t: (0, 0)),
            pl.BlockSpec((1, D_MODEL, wcols), lambda b, t: (l, 0, 0)),
            pl.BlockSpec((SC_KERNEL, BRANCH_W), lambda b, t: (0, 0)),
            pl.BlockSpec((CF_KERNEL, BRANCH_W), lambda b, t: (0, 0)),
            vec(), vec(), vec(),
        ],
        out_specs=[pl.BlockSpec((1, T, BRANCH_W), lambda b, t: (b, t, 0)),
                   pl.BlockSpec((1, T, BRANCH_W), lambda b, t: (b, t, 0))],
        out_shape=[jax.ShapeDtypeStruct((B, S, BRANCH_W), BF16),
                   jax.ShapeDtypeStruct((B, S, BRANCH_W), BF16)],
        scratch_shapes=[
            pltpu.VMEM((T + 2 * HL, D_MODEL), BF16),
            pltpu.VMEM((T + 2 * HL, BRANCH_W), F32),
            pltpu.VMEM((T + 2 * HL, BRANCH_W), F32),
        ],
        compiler_params=_params(),
        name="conv_branches",
    )(xs, xs, xs, mod, nw, w_conv, scw, cfw, cfb, lnw, lnb)


def _merge_kernel(final, x_ref, mod_ref, nw_ref, wg_ref, r_ref, s_ref, f_ref, a_ref, wb_ref, wo_ref,
                  fnw_ref, o_ref):
    x = x_ref[0]
    h = _normed(x, mod_ref, nw_ref)
    merged = None
    for i, br in enumerate((r_ref, s_ref, f_ref, a_ref)):
        g = _dot(h, wg_ref[0, :, i * D_MODEL:(i + 1) * D_MODEL])
        y = _sigmoid(g) * _dot(br[0], wb_ref[0, i])
        merged = y if merged is None else merged + y
    out = _dot(merged.astype(BF16), wo_ref[0])
    xn = x + mod_ref[0, 2:3, :] * out
    if final:
        ms = jnp.mean(xn * xn, axis=-1, keepdims=True)
        xn = xn * lax.rsqrt(ms + EPS) * fnw_ref[...]
    o_ref[0] = xn


def _merge(xs, mod, nw, w_gate, l, r, s, f, a, w_branch, w_out, fnw, T, final):
    B, S, _ = xs.shape
    nT = S // T
    br = lambda: pl.BlockSpec((1, T, BRANCH_W), lambda b, t: (b, t, 0))
    return pl.pallas_call(
        functools.partial(_merge_kernel, final),
        grid=(B, nT),
        in_specs=[
            pl.BlockSpec((1, T, D_MODEL), lambda b, t: (b, t, 0)),
            pl.BlockSpec((1, 3, D_MODEL), lambda b, t: (b, 0, 0)),
            pl.BlockSpec((1, D_MODEL), lambda b, t: (0, 0)),
            pl.BlockSpec((1, D_MODEL, 4 * D_MODEL), lambda b, t: (l, 0, 0)),
            br(), br(), br(), br(),
            pl.BlockSpec((1, 4, BRANCH_W, D_MODEL), lambda b, t: (l, 0, 0, 0)),
            pl.BlockSpec((1, D_MODEL, D_MODEL), lambda b, t: (l, 0, 0)),
            pl.BlockSpec((1, D_MODEL), lambda b, t: (0, 0)),
        ],
        out_specs=pl.BlockSpec((1, T, D_MODEL), lambda b, t: (b, t, 0)),
        out_shape=jax.ShapeDtypeStruct((B, S, D_MODEL), F32),
        compiler_params=_params(),
        name="merge",
    )(xs, mod, nw, w_gate, r, s, f, a, w_branch, w_out, fnw)


def _rope_table(S):
    rows = S // GRID_W
    row = jnp.repeat(jnp.arange(rows), GRID_W).astype(F32)
    col = jnp.tile(jnp.arange(GRID_W), rows).astype(F32)
    inv = ROPE_BASE ** (-jnp.arange(ROPE_AXIS_FREQS, dtype=F32) / ROPE_AXIS_FREQS)
    ang = jnp.concatenate([row[:, None] * inv[None], col[:, None] * inv[None]], axis=-1)
    cos, sin = jnp.cos(ang), jnp.sin(ang)
    zero = jnp.zeros_like(sin)
    reps = LANES // ATT_DH
    cos_t = jnp.tile(jnp.concatenate([cos, cos], axis=-1), (1, reps))
    sin_lo = jnp.tile(jnp.concatenate([-sin, zero], axis=-1), (1, reps))
    sin_hi = jnp.tile(jnp.concatenate([zero, sin], axis=-1), (1, reps))
    return jnp.stack([cos_t, sin_lo, sin_hi])


def kernel(x, c, ctx, c_ctx, w_mod, b_mod, norm_w, w_in, ret_decay, ret_gn_w, sc_conv_w, cf_conv_w,
           cf_conv_b, cf_ln_w, cf_ln_b, att_sink, w_branch, w_out, final_norm_w):
    B, S, D = x.shape
    L = ctx.shape[1]
    T = TILE_X

    w_ret = w_in[:, :, RET_COLS[0]:RET_COLS[1]].astype(BF16)
    w_conv = w_in[:, :, CONV_COLS[0]:CONV_COLS[1]].astype(BF16)
    w_att = w_in[:, :, ATT_COLS[0]:ATT_COLS[1]].astype(BF16)
    w_gate = w_in[:, :, GATE_COLS[0]:GATE_COLS[1]].astype(BF16)
    w_br = w_branch.astype(BF16)
    w_o = w_out.astype(BF16)
    rope_tab = _rope_table(S)

    mod_rows = 16
    cc = jnp.concatenate([c, c_ctx[None], jnp.zeros((mod_rows - B - 1, D), F32)], axis=0)
    mod_all = _modulation(cc, w_mod, b_mod)
    mod_x = mod_all[:, :B].reshape(DEPTH, B, 3, D)
    mod_c = jnp.broadcast_to(mod_all[:, B:B + 1].reshape(DEPTH, 1, 3, D), (DEPTH, B, 3, D))

    zero_state = jnp.zeros((B, RET_HEADS, RET_DH, RET_DH), F32)
    xc = ctx
    for l in range(DEPTH):
        last = l == DEPTH - 1
        nw = norm_w[l][None]
        dec = jnp.broadcast_to(ret_decay[l].reshape(2 * RET_HEADS, 1), (2 * RET_HEADS, LANES))
        gn = ret_gn_w[l][None]
        vecs = (sc_conv_w[l], cf_conv_w[l], cf_conv_b[l][None], cf_ln_w[l][None], cf_ln_b[l][None])
        fnw = final_norm_w[None]

        r_c, s_f, s_b = _retention(xc, mod_c[l], nw, w_ret, l, dec, gn, zero_state, zero_state, L)
        a_c, k_c, v_c = _attention_ctx(xc, mod_c[l], nw, w_att, l, att_sink[l])

        r_x, _, _ = _retention(x, mod_x[l], nw, w_ret, l, dec, gn, s_f, s_b, T)
        a_x = _attention_local(x, mod_x[l], nw, w_att, l, rope_tab, k_c, v_c, att_sink[l], T)
        s_x, f_x = _conv_branches(x, mod_x[l], nw, w_conv, l, *vecs, T)
        x = _merge(x, mod_x[l], nw, w_gate, l, r_x, s_x, f_x, a_x, w_br, w_o, fnw, T, last)

        if not last:
            s_c, f_c = _conv_branches(xc, mod_c[l], nw, w_conv, l, *vecs, L)
            xc = _merge(xc, mod_c[l], nw, w_gate, l, r_c, s_c, f_c, a_c, w_br, w_o, fnw, L, False)
    return x
```

```python
import functools

import numpy as np
import jax
import jax.numpy as jnp
from jax import lax
from jax.experimental import pallas as pl
from jax.experimental.pallas import tpu as pltpu

D_MODEL = 1024
DEPTH = 4
GRID_W = 64
BRANCH_W = D_MODEL // 2
RET_HEADS = 4
RET_DH = BRANCH_W // RET_HEADS
RET_CHUNK = 128
SC_KERNEL = 3
CF_KERNEL = 31
ATT_HEADS = 8
ATT_KV_HEADS = 2
ATT_GROUP = ATT_HEADS // ATT_KV_HEADS
ATT_DH = BRANCH_W // ATT_HEADS
ATT_KV_W = ATT_KV_HEADS * ATT_DH
ATT_BLOCK = 128
WINDOW = 128
ROPE_BASE = 10000.0
ROPE_AXIS_FREQS = ATT_DH // 4
EPS = 1e-6
NEG_INF = -1e30

RET_COLS = (0, 4 * BRANCH_W)
CONV_COLS = (4 * BRANCH_W, 11 * BRANCH_W)
ATT_COLS = (11 * BRANCH_W, 13 * BRANCH_W + 2 * ATT_KV_W)
GATE_COLS = (13 * BRANCH_W + 2 * ATT_KV_W, 13 * BRANCH_W + 2 * ATT_KV_W + 4 * D_MODEL)

LANES = 128
SUBLANES = 8
CONV_HALO = 16
TILE_X = 512
VMEM_LIMIT = 56 * 1024 * 1024

F32 = jnp.float32
BF16 = jnp.bfloat16


def _dot(a, b):
    return jnp.dot(a, b, preferred_element_type=F32)


def _dot_nt(a, b):
    return lax.dot_general(a, b, (((1,), (1,)), ((), ())), preferred_element_type=F32)


def _dot_tn(a, b):
    return lax.dot_general(a, b, (((0,), (0,)), ((), ())), preferred_element_type=F32)


def _sigmoid(v):
    return jax.nn.sigmoid(v)


def _silu(v):
    return v * _sigmoid(v)


def _normed(x, mod_ref, nw_ref):
    a = nw_ref[...] * (1.0 + mod_ref[0, 1:2, :])
    shift = mod_ref[0, 0:1, :]
    ms = jnp.mean(x * x, axis=-1, keepdims=True)
    return (x * lax.rsqrt(ms + EPS) * a + shift).astype(BF16)


def _params():
    return pltpu.CompilerParams(vmem_limit_bytes=VMEM_LIMIT)


def _mod_kernel(c_ref, w_ref, b_ref, o_ref):
    cv = c_ref[...]
    o_ref[0] = _dot(_silu(cv).astype(BF16), w_ref[0].astype(BF16)) + b_ref[0]


def _modulation(cc, w_mod, b_mod):
    rows = cc.shape[0]
    tn = 1024
    return pl.pallas_call(
        _mod_kernel,
        grid=(DEPTH, 3 * D_MODEL // tn),
        in_specs=[pl.BlockSpec((rows, D_MODEL), lambda l, n: (0, 0)),
                  pl.BlockSpec((1, D_MODEL, tn), lambda l, n: (l, 0, n)),
                  pl.BlockSpec((1, 1, tn), lambda l, n: (l, 0, n))],
        out_specs=pl.BlockSpec((1, rows, tn), lambda l, n: (l, 0, n)),
        out_shape=jax.ShapeDtypeStruct((DEPTH, rows, 3 * D_MODEL), F32),
        compiler_params=_params(),
        name="modulation",
    )(cc, w_mod, b_mod.reshape(DEPTH, 1, 3 * D_MODEL))


_T_DEC, _T_QF, _T_KF, _T_QB, _T_KB, _T_CF, _T_CB = range(7)


def _ret_kernel(T, S, x_ref, mod_ref, nw_ref, w_ref, dec_ref, gn_ref, s0f_ref, s0b_ref,
                r_ref, sf_ref, sb_ref, qkv, zbuf, obuf, st_f, st_b, tab):
    p = pl.program_id(1)
    t = pl.program_id(2)
    nT = S // T
    C = RET_CHUNK
    H = RET_HEADS
    k_scale = RET_DH ** -0.5

    @pl.when((p == 0) & (t == 0))
    def _init():
        raw = dec_ref[...]
        lg = jnp.minimum(raw, 0.0) - jnp.log(1.0 + jnp.exp(-jnp.abs(raw)))
        ii = lax.broadcasted_iota(jnp.int32, (C, C), 0)
        jj = lax.broadcasted_iota(jnp.int32, (C, C), 1)
        diff = (ii - jj).astype(F32)
        pos = ii.astype(F32)
        for h in range(H):
            lf = lg[h:h + 1, :]
            lb = lg[H + h:H + h + 1, :]
            dec_f = jnp.where(diff >= 0, jnp.exp(jnp.where(diff >= 0, diff, 0.0) * lf), 0.0)
            dec_b = jnp.where(diff < 0, jnp.exp(jnp.where(diff < 0, -diff, 0.0) * lb), 0.0)
            tab[_T_DEC * H + h] = (dec_f + dec_b) * k_scale
            tab[_T_QF * H + h] = jnp.exp((pos + 1.0) * lf)
            tab[_T_KF * H + h] = jnp.exp((C - 1.0 - pos) * lf) * k_scale
            tab[_T_QB * H + h] = jnp.exp((C - pos) * lb)
            tab[_T_KB * H + h] = jnp.exp(pos * lb) * k_scale
            tab[_T_CF * H + h] = jnp.exp(float(C) * lf) + jnp.zeros((C, LANES), F32)
            tab[_T_CB * H + h] = jnp.exp(float(C) * lb) + jnp.zeros((C, LANES), F32)
        st_f[...] = s0f_ref[0]
        st_b[...] = s0b_ref[0]

    def head_cols(part, h):
        return slice(part * BRANCH_W + h * RET_DH, part * BRANCH_W + (h + 1) * RET_DH)

    @pl.when(p == 0)
    def _forward():
        h_tile = _normed(x_ref[0], mod_ref, nw_ref)
        proj = _dot(h_tile, w_ref[0])
        row0 = pl.multiple_of(t * T, T)
        qkv[pl.ds(row0, T), :] = proj[:, :3 * BRANCH_W].astype(BF16)
        zbuf[pl.ds(row0, T), :] = proj[:, 3 * BRANCH_W:]
        for c in range(T // C):
            rows = pl.ds(pl.multiple_of(t * T + c * C, C), C)
            for h in range(H):
                q = qkv[rows, head_cols(0, h)]
                k = qkv[rows, head_cols(1, h)]
                v = qkv[rows, head_cols(2, h)]
                sc = (_dot_nt(q, k) * tab[_T_DEC * H + h]).astype(BF16)
                qs = (q.astype(F32) * tab[_T_QF * H + h]).astype(BF16)
                kwt = jnp.transpose(k.astype(F32) * tab[_T_KF * H + h]).astype(BF16)
                lhs = jnp.concatenate([jnp.concatenate([sc, qs], axis=1),
                                       jnp.concatenate([kwt, jnp.zeros((RET_DH, RET_DH), BF16)], axis=1)], axis=0)
                res = _dot(lhs, jnp.concatenate([v, st_f[h].astype(BF16)], axis=0))
                obuf[rows, h * RET_DH:(h + 1) * RET_DH] = res[:C]
                st_f[h] = st_f[h] * tab[_T_CF * H + h] + res[C:]

        @pl.when(t == nT - 1)
        def _():
            sf_ref[0] = st_f[...]

    @pl.when(p == 1)
    def _backward():
        tt = nT - 1 - t
        for c in reversed(range(T // C)):
            rows = pl.ds(pl.multiple_of(tt * T + c * C, C), C)
            for h in range(H):
                hc = slice(h * RET_DH, (h + 1) * RET_DH)
                q = qkv[rows, head_cols(0, h)]
                k = qkv[rows, head_cols(1, h)]
                v = qkv[rows, head_cols(2, h)]
                qs = (q.astype(F32) * tab[_T_QB * H + h]).astype(BF16)
                kwt = jnp.transpose(k.astype(F32) * tab[_T_KB * H + h]).astype(BF16)
                zero = jnp.zeros((C, RET_DH), BF16)
                lhs = jnp.concatenate([jnp.concatenate([qs, zero], axis=1),
                                       jnp.concatenate([zero, kwt], axis=1)], axis=0)
                res = _dot(lhs, jnp.concatenate([st_b[h].astype(BF16), v], axis=0))
                o = obuf[rows, hc] + res[:C]
                mu = jnp.mean(o, axis=-1, keepdims=True)
                d = o - mu
                var = jnp.mean(d * d, axis=-1, keepdims=True)
                y = d * lax.rsqrt(var + EPS) * gn_ref[:, hc]
                r_ref[0, c * C:(c + 1) * C, hc] = (y * _silu(zbuf[rows, hc])).astype(BF16)
                st_b[h] = st_b[h] * tab[_T_CB * H + h] + res[C:]

        @pl.when(t == nT - 1)
        def _():
            sb_ref[0] = st_b[...]


def _retention(xs, mod, nw, w_ret, l, dec, gn_w, s0f, s0b, T):
    B, S, _ = xs.shape
    nT = S // T
    st_shape = (RET_HEADS, RET_DH, RET_DH)
    wcols = RET_COLS[1] - RET_COLS[0]
    return pl.pallas_call(
        functools.partial(_ret_kernel, T, S),
        grid=(B, 2, nT),
        in_specs=[
            pl.BlockSpec((1, T, D_MODEL), lambda b, p, t: (b, t * (1 - p) + (nT - 1) * p, 0)),
            pl.BlockSpec((1, 3, D_MODEL), lambda b, p, t: (b, 0, 0)),
            pl.BlockSpec((1, D_MODEL), lambda b, p, t: (0, 0)),
            pl.BlockSpec((1, D_MODEL, wcols), lambda b, p, t: (l, 0, 0)),
            pl.BlockSpec((2 * RET_HEADS, LANES), lambda b, p, t: (0, 0)),
            pl.BlockSpec((1, BRANCH_W), lambda b, p, t: (0, 0)),
            pl.BlockSpec((1,) + st_shape, lambda b, p, t: (b, 0, 0, 0)),
            pl.BlockSpec((1,) + st_shape, lambda b, p, t: (b, 0, 0, 0)),
        ],
        out_specs=[
            pl.BlockSpec((1, T, BRANCH_W), lambda b, p, t: (b, nT - 1 - t * p, 0)),
            pl.BlockSpec((1,) + st_shape, lambda b, p, t: (b, 0, 0, 0)),
            pl.BlockSpec((1,) + st_shape, lambda b, p, t: (b, 0, 0, 0)),
        ],
        out_shape=[
            jax.ShapeDtypeStruct((B, S, BRANCH_W), BF16),
            jax.ShapeDtypeStruct((B,) + st_shape, F32),
            jax.ShapeDtypeStruct((B,) + st_shape, F32),
        ],
        scratch_shapes=[
            pltpu.VMEM((S, 3 * BRANCH_W), BF16),
            pltpu.VMEM((S, BRANCH_W), F32),
            pltpu.VMEM((S, BRANCH_W), F32),
            pltpu.VMEM(st_shape, F32),
            pltpu.VMEM(st_shape, F32),
            pltpu.VMEM((7 * RET_HEADS, RET_CHUNK, LANES), F32),
        ],
        compiler_params=_params(),
        name="retention",
    )(xs, mod, nw, w_ret, dec, gn_w, s0f, s0b)


def _rope(val, rope_ref, r0, nrows):
    cos = rope_ref[0, pl.ds(r0, nrows), :]
    sin_lo = rope_ref[1, pl.ds(r0, nrows), :]
    sin_hi = rope_ref[2, pl.ds(r0, nrows), :]
    outs = []
    for g in range(val.shape[1] // LANES):
        vg = val[:, g * LANES:(g + 1) * LANES]
        outs.append(vg * cos + pltpu.roll(vg, LANES - ATT_DH // 2, 1) * sin_lo
                    + pltpu.roll(vg, ATT_DH // 2, 1) * sin_hi)
    return outs[0] if len(outs) == 1 else jnp.concatenate(outs, axis=1)


def _softmax_av(qh, parts, sink):
    scores = []
    m = None
    for k, _, mask in parts:
        s = _dot_nt(qh, k)
        if mask is not None:
            s = jnp.where(mask, s, NEG_INF)
        scores.append(s)
        sm = jnp.max(s, axis=-1, keepdims=True)
        m = sm if m is None else jnp.maximum(m, sm)
    m = jnp.maximum(m, sink)
    den = jnp.exp(sink - m)
    o = None
    for s, (_, v, _) in zip(scores, parts):
        pr = jnp.exp(s - m)
        den = den + jnp.sum(pr, axis=-1, keepdims=True)
        ov = _dot(pr.astype(BF16), v)
        o = ov if o is None else o + ov
    return o / den


def _v_ext(v, hk):
    lane = lax.broadcasted_iota(jnp.int32, v.shape, 1)
    vh = v if hk == 0 else pltpu.roll(v, ATT_DH, 1)
    return jnp.where(lane < ATT_DH, vh, 1.0).astype(BF16)


def _att_local_kernel(T, S, x_ref, xp_ref, xn_ref, mod_ref, nw_ref, w_ref, rope_ref, kc_ref, vc_ref,
                      sink_ref, a_ref, q_s, k_s, v_s, z_s):
    t = pl.program_id(1)
    nb = T // ATT_BLOCK
    blk = ATT_BLOCK
    G = ATT_GROUP
    q_cols = slice(0, BRANCH_W)
    kv_cols = slice(BRANCH_W, BRANCH_W + 2 * ATT_KV_W)
    z_cols = slice(BRANCH_W + 2 * ATT_KV_W, 2 * BRANCH_W + 2 * ATT_KV_W)

    def put_kv(kv, r0, dst):
        k = _rope(kv[:, :ATT_KV_W], rope_ref, r0, kv.shape[0])
        for hk in range(ATT_KV_HEADS):
            k_s[hk, dst, :] = k[:, hk * ATT_DH:(hk + 1) * ATT_DH].astype(BF16)
            v_s[hk, dst, :] = _v_ext(kv[:, ATT_KV_W:], hk)

    hm = _normed(x_ref[0], mod_ref, nw_ref)
    row0 = pl.multiple_of(t * T, T)
    q = _rope(_dot(hm, w_ref[0, :, q_cols]), rope_ref, row0, T) * (ATT_DH ** -0.5)
    for j in range(nb):
        for hd in range(ATT_HEADS):
            hk, g = divmod(hd, G)
            q_s[j * ATT_KV_HEADS + hk, g * blk:(g + 1) * blk, :] = (
                q[j * blk:(j + 1) * blk, hd * ATT_DH:(hd + 1) * ATT_DH].astype(BF16))
    put_kv(_dot(hm, w_ref[0, :, kv_cols]), row0, slice(blk, blk + T))
    z_s[...] = _silu(_dot(hm, w_ref[0, :, z_cols]))

    rp = pl.multiple_of(jnp.maximum(t * T - blk, 0), blk)
    put_kv(_dot(_normed(xp_ref[0], mod_ref, nw_ref), w_ref[0, :, kv_cols]), rp, slice(0, blk))
    rn = pl.multiple_of(jnp.minimum((t + 1) * T, S - blk), blk)
    put_kv(_dot(_normed(xn_ref[0], mod_ref, nw_ref), w_ref[0, :, kv_cols]), rn, slice(blk + T, 2 * blk + T))

    qi = lax.broadcasted_iota(jnp.int32, (G * blk, blk), 0) & (blk - 1)
    kj = lax.broadcasted_iota(jnp.int32, (G * blk, blk), 1)
    grp = lax.broadcasted_iota(jnp.int32, (G * blk, 1), 0) // blk
    lane = lax.broadcasted_iota(jnp.int32, (blk, LANES), 1)
    lane4 = lax.broadcasted_iota(jnp.int32, (G * blk, LANES), 1)
    for hk in range(ATT_KV_HEADS):
        sink = jnp.full((G * blk, 1), sink_ref[hk * G], F32)
        for g in range(1, G):
            sink = jnp.where(grp == g, sink_ref[hk * G + g], sink)
        for j in range(nb):
            gb = t * nb + j
            wrows = slice(j * blk, (j + 3) * blk)
            q4 = q_s[j * ATT_KV_HEADS + hk]
            s_loc = _dot_nt(q4, k_s[hk, wrows, :])
            s_prev = jnp.where((kj >= qi) & (gb > 0), s_loc[:, :blk], NEG_INF)
            s_cur = s_loc[:, blk:2 * blk]
            s_next = jnp.where((kj <= qi) & (gb < S // blk - 1), s_loc[:, 2 * blk:], NEG_INF)
            s_ctx = _dot_nt(q4, kc_ref[0, hk])
            m = jnp.maximum(jnp.maximum(s_prev, s_cur), s_next)
            for cb in range(s_ctx.shape[1] // blk):
                m = jnp.maximum(m, s_ctx[:, cb * blk:(cb + 1) * blk])
            m = jnp.maximum(jnp.max(m, axis=-1, keepdims=True), sink)
            p_loc = jnp.concatenate([jnp.exp(s_prev - m), jnp.exp(s_cur - m), jnp.exp(s_next - m)], axis=1)
            o = (_dot(p_loc.astype(BF16), v_s[hk, wrows, :])
                 + _dot(jnp.exp(s_ctx - m).astype(BF16), vc_ref[0, hk]))
            o = o + jnp.where(lane4 >= ATT_DH, jnp.exp(sink - m), 0.0)
            qrows = slice(j * blk, (j + 1) * blk)
            for pair in range(G // 2):
                oe = o[(2 * pair) * blk:(2 * pair + 1) * blk]
                od = o[(2 * pair + 1) * blk:(2 * pair + 2) * blk]
                num = jnp.where(lane < ATT_DH, oe, pltpu.roll(od, ATT_DH, 1))
                den = jnp.where(lane < ATT_DH, pltpu.roll(oe, ATT_DH, 1), od)
                cols = slice((hk * G + 2 * pair) * ATT_DH, (hk * G + 2 * pair + 2) * ATT_DH)
                a_ref[0, qrows, cols] = (num / den * z_s[qrows, cols]).astype(BF16)


def _att_ctx_kernel(x_ref, mod_ref, nw_ref, w_ref, sink_ref, a_ref, kc_ref, vc_ref, q_s, z_s):
    q_cols = slice(0, BRANCH_W)
    kv_cols = slice(BRANCH_W, BRANCH_W + 2 * ATT_KV_W)
    z_cols = slice(BRANCH_W + 2 * ATT_KV_W, 2 * BRANCH_W + 2 * ATT_KV_W)
    hm = _normed(x_ref[0], mod_ref, nw_ref)
    q_s[...] = (_dot(hm, w_ref[0, :, q_cols]) * (ATT_DH ** -0.5)).astype(BF16)
    kv = _dot(hm, w_ref[0, :, kv_cols])
    for hk in range(ATT_KV_HEADS):
        kc_ref[0, hk] = kv[:, hk * ATT_DH:(hk + 1) * ATT_DH].astype(BF16)
        vc_ref[0, hk] = _v_ext(kv[:, ATT_KV_W:], hk)
    z_s[...] = _silu(_dot(hm, w_ref[0, :, z_cols]))
    for hk in range(ATT_KV_HEADS):
        parts = [(kc_ref[0, hk], vc_ref[0, hk, :, 0:ATT_DH], None)]
        for g in range(ATT_GROUP):
            hd = hk * ATT_GROUP + g
            hc = slice(hd * ATT_DH, (hd + 1) * ATT_DH)
            o = _softmax_av(q_s[:, hc], parts, sink_ref[hd])
            a_ref[0, :, hc] = (o * z_s[:, hc]).astype(BF16)


def _attention_local(xs, mod, nw, w_att, l, rope_tab, kc, vc, sink, T):
    B, S, _ = xs.shape
    nT = S // T
    bpt = T // ATT_BLOCK
    nblk = S // ATT_BLOCK
    L = kc.shape[2]
    wcols = ATT_COLS[1] - ATT_COLS[0]
    return pl.pallas_call(
        functools.partial(_att_local_kernel, T, S),
        grid=(B, nT),
        in_specs=[
            pl.BlockSpec((1, T, D_MODEL), lambda b, t: (b, t, 0)),
            pl.BlockSpec((1, ATT_BLOCK, D_MODEL), lambda b, t: (b, jnp.maximum(t * bpt - 1, 0), 0)),
            pl.BlockSpec((1, ATT_BLOCK, D_MODEL), lambda b, t: (b, jnp.minimum((t + 1) * bpt, nblk - 1), 0)),
            pl.BlockSpec((1, 3, D_MODEL), lambda b, t: (b, 0, 0)),
            pl.BlockSpec((1, D_MODEL), lambda b, t: (0, 0)),
            pl.BlockSpec((1, D_MODEL, wcols), lambda b, t: (l, 0, 0)),
            pl.BlockSpec((3, S, LANES), lambda b, t: (0, 0, 0)),
            pl.BlockSpec((1, ATT_KV_HEADS, L, ATT_DH), lambda b, t: (b, 0, 0, 0)),
            pl.BlockSpec((1, ATT_KV_HEADS, L, LANES), lambda b, t: (b, 0, 0, 0)),
            pl.BlockSpec(memory_space=pltpu.SMEM),
        ],
        out_specs=pl.BlockSpec((1, T, BRANCH_W), lambda b, t: (b, t, 0)),
        out_shape=jax.ShapeDtypeStruct((B, S, BRANCH_W), BF16),
        scratch_shapes=[
            pltpu.VMEM((bpt * ATT_KV_HEADS, ATT_GROUP * ATT_BLOCK, ATT_DH), BF16),
            pltpu.VMEM((ATT_KV_HEADS, T + 2 * ATT_BLOCK, ATT_DH), BF16),
            pltpu.VMEM((ATT_KV_HEADS, T + 2 * ATT_BLOCK, LANES), BF16),
            pltpu.VMEM((T, BRANCH_W), F32),
        ],
        compiler_params=_params(),
        name="attention_local",
    )(xs, xs, xs, mod, nw, w_att, rope_tab, kc, vc, sink)


def _attention_ctx(xc, mod, nw, w_att, l, sink):
    B, L, _ = xc.shape
    wcols = ATT_COLS[1] - ATT_COLS[0]
    return pl.pallas_call(
        _att_ctx_kernel,
        grid=(B,),
        in_specs=[
            pl.BlockSpec((1, L, D_MODEL), lambda b: (b, 0, 0)),
            pl.BlockSpec((1, 3, D_MODEL), lambda b: (b, 0, 0)),
            pl.BlockSpec((1, D_MODEL), lambda b: (0, 0)),
            pl.BlockSpec((1, D_MODEL, wcols), lambda b: (l, 0, 0)),
            pl.BlockSpec(memory_space=pltpu.SMEM),
        ],
        out_specs=[
            pl.BlockSpec((1, L, BRANCH_W), lambda b: (b, 0, 0)),
            pl.BlockSpec((1, ATT_KV_HEADS, L, ATT_DH), lambda b: (b, 0, 0, 0)),
            pl.BlockSpec((1, ATT_KV_HEADS, L, LANES), lambda b: (b, 0, 0, 0)),
        ],
        out_shape=[
            jax.ShapeDtypeStruct((B, L, BRANCH_W), BF16),
            jax.ShapeDtypeStruct((B, ATT_KV_HEADS, L, ATT_DH), BF16),
            jax.ShapeDtypeStruct((B, ATT_KV_HEADS, L, LANES), BF16),
        ],
        scratch_shapes=[
            pltpu.VMEM((L, BRANCH_W), BF16),
            pltpu.VMEM((L, BRANCH_W), F32),
        ],
        compiler_params=_params(),
        name="attention_ctx",
    )(xc, mod, nw, w_att, sink)


def _sc_phases():
    return tuple(sorted({(CONV_HALO - SC_KERNEL // 2 + k) % SUBLANES for k in range(SC_KERNEL)}))


def _conv_kernel(T, S, x_ref, xp_ref, xn_ref, mod_ref, nw_ref, w_ref, scw_ref, cfw_ref, cfb_ref,
                 lnw_ref, lnb_ref, s_ref, f_ref, hbuf, u_sc, u_cf, wrep):
    t = pl.program_id(1)
    nT = S // T
    HL = CONV_HALO
    W = BRANCH_W
    E = T + 2 * HL
    SUB = SUBLANES
    sc_phases = _sc_phases()
    hbuf[0:HL, :] = _normed(xp_ref[0], mod_ref, nw_ref)
    hbuf[HL:HL + T, :] = _normed(x_ref[0], mod_ref, nw_ref)
    hbuf[HL + T:2 * HL + T, :] = _normed(xn_ref[0], mod_ref, nw_ref)
    hext = hbuf[...]
    hm = hbuf[HL:HL + T, :]

    row = lax.broadcasted_iota(jnp.int32, (E, 1), 0)
    in_seq = ((row >= HL) | (t > 0)) & ((row < HL + T) | (t < nT - 1))

    def store_phases(u, dst, phases):
        for i, ph in enumerate(phases):
            dst[i, 0:E - SUB, :] = u[ph:ph + E - SUB, :]

    cx = _dot(hext, w_ref[0, :, W:3 * W])
    store_phases(jnp.where(in_seq, cx[:, :W] * cx[:, W:], 0.0), u_sc, sc_phases)
    glu = _dot(hext, w_ref[0, :, 4 * W:6 * W])
    store_phases(jnp.where(in_seq, glu[:, :W] * _sigmoid(glu[:, W:]), 0.0), u_cf, range(SUB))

    b_gate = _dot(hm, w_ref[0, :, 0:W])
    z_sc = _dot(hm, w_ref[0, :, 3 * W:4 * W])
    z_cf = _dot(hm, w_ref[0, :, 6 * W:7 * W])

    vec_rows = ([scw_ref[k:k + 1, :] for k in range(SC_KERNEL)] + [cfw_ref[k:k + 1, :] for k in range(CF_KERNEL)]
                + [cfb_ref[...], lnw_ref[...], lnb_ref[...]])
    for i, v in enumerate(vec_rows):
        wrep[i] = jnp.broadcast_to(v, (SUB, W))
    i_cfb, i_lnw, i_lnb = (SC_KERNEL + CF_KERNEL + i for i in range(3))

    RB = 64

    def rows3(v):
        return v.reshape(RB // SUB, SUB, W)

    for rb in range(T // RB):
        r = rb * RB
        acc = None
        for k in range(SC_KERNEL):
            a, ph = divmod(HL - SC_KERNEL // 2 + k, SUB)
            term = wrep[k] * rows3(u_sc[sc_phases.index(ph), r + SUB * a:r + SUB * a + RB, :])
            acc = term if acc is None else acc + term
        acc = acc.reshape(RB, W)
        s_ref[0, r:r + RB, :] = (b_gate[r:r + RB] * acc * _silu(z_sc[r:r + RB])).astype(BF16)

        acc = wrep[i_cfb]
        for k in range(CF_KERNEL):
            a, ph = divmod(HL - CF_KERNEL // 2 + k, SUB)
            acc = acc + wrep[SC_KERNEL + k] * rows3(u_cf[ph, r + SUB * a:r + SUB * a + RB, :])
        acc = acc.reshape(RB, W)
        mu = jnp.mean(acc, axis=-1, keepdims=True)
        d = acc - mu
        var = jnp.mean(d * d, axis=-1, keepdims=True)
        y = (rows3(d * lax.rsqrt(var + EPS)) * wrep[i_lnw] + wrep[i_lnb]).reshape(RB, W)
        f_ref[0, r:r + RB, :] = (_silu(y) * _silu(z_cf[r:r + RB])).astype(BF16)


def _conv_branches(xs, mod, nw, w_conv, l, scw, cfw, cfb, lnw, lnb, T):
    B, S, _ = xs.shape
    nT = S // T
    HL = CONV_HALO
    bpt = T // HL
    nblk = S // HL
    wcols = CONV_COLS[1] - CONV_COLS[0]
    vec = lambda: pl.BlockSpec((1, BRANCH_W), lambda b, t: (0, 0))
    return pl.pallas_call(
        functools.partial(_conv_kernel, T, S),
        grid=(B, nT),
        in_specs=[
            pl.BlockSpec((1, T, D_MODEL), lambda b, t: (b, t, 0)),
            pl.BlockSpec((1, HL, D_MODEL), lambda b, t: (b, jnp.maximum(t * bpt - 1, 0), 0)),
            pl.BlockSpec((1, HL, D_MODEL), lambda b, t: (b, jnp.minimum((t + 1) * bpt, nblk - 1), 0)),
            pl.BlockSpec((1, 3, D_MODEL), lambda b, t: (b, 0, 0)),
            pl.BlockSpec((1, D_MODEL), lambda b, t: (0, 0)),
            pl.BlockSpec((1, D_MODEL, wcols), lambda b, t: (l, 0, 0)),
            pl.BlockSpec((SC_KERNEL, BRANCH_W), lambda b, t: (0, 0)),
            pl.BlockSpec((CF_KERNEL, BRANCH_W), lambda b, t: (0, 0)),
            vec(), vec(), vec(),
        ],
        out_specs=[pl.BlockSpec((1, T, BRANCH_W), lambda b, t: (b, t, 0)),
                   pl.BlockSpec((1, T, BRANCH_W), lambda b, t: (b, t, 0))],
        out_shape=[jax.ShapeDtypeStruct((B, S, BRANCH_W), BF16),
                   jax.ShapeDtypeStruct((B, S, BRANCH_W), BF16)],
        scratch_shapes=[
            pltpu.VMEM((T + 2 * HL, D_MODEL), BF16),
            pltpu.VMEM((len(_sc_phases()), T + 2 * HL, BRANCH_W), F32),
            pltpu.VMEM((SUBLANES, T + 2 * HL, BRANCH_W), F32),
            pltpu.VMEM((SC_KERNEL + CF_KERNEL + 3, SUBLANES, BRANCH_W), F32),
        ],
        compiler_params=_params(),
        name="conv_branches",
    )(xs, xs, xs, mod, nw, w_conv, scw, cfw, cfb, lnw, lnb)


def _merge_kernel(final, x_ref, mod_ref, nw_ref, wg_ref, r_ref, s_ref, f_ref, a_ref, wb_ref, wo_ref,
                  fnw_ref, o_ref):
    x = x_ref[0]
    h = _normed(x, mod_ref, nw_ref)
    merged = None
    for i, br in enumerate((r_ref, s_ref, f_ref, a_ref)):
        g = _dot(h, wg_ref[0, :, i * D_MODEL:(i + 1) * D_MODEL])
        y = _sigmoid(g) * _dot(br[0], wb_ref[0, i])
        merged = y if merged is None else merged + y
    out = _dot(merged.astype(BF16), wo_ref[0])
    xn = x + mod_ref[0, 2:3, :] * out
    if final:
        ms = jnp.mean(xn * xn, axis=-1, keepdims=True)
        xn = xn * lax.rsqrt(ms + EPS) * fnw_ref[...]
    o_ref[0] = xn


def _merge(xs, mod, nw, w_gate, l, r, s, f, a, w_branch, w_out, fnw, T, final):
    B, S, _ = xs.shape
    nT = S // T
    br = lambda: pl.BlockSpec((1, T, BRANCH_W), lambda b, t: (b, t, 0))
    return pl.pallas_call(
        functools.partial(_merge_kernel, final),
        grid=(B, nT),
        in_specs=[
            pl.BlockSpec((1, T, D_MODEL), lambda b, t: (b, t, 0)),
            pl.BlockSpec((1, 3, D_MODEL), lambda b, t: (b, 0, 0)),
            pl.BlockSpec((1, D_MODEL), lambda b, t: (0, 0)),
            pl.BlockSpec((1, D_MODEL, 4 * D_MODEL), lambda b, t: (l, 0, 0)),
            br(), br(), br(), br(),
            pl.BlockSpec((1, 4, BRANCH_W, D_MODEL), lambda b, t: (l, 0, 0, 0)),
            pl.BlockSpec((1, D_MODEL, D_MODEL), lambda b, t: (l, 0, 0)),
            pl.BlockSpec((1, D_MODEL), lambda b, t: (0, 0)),
        ],
        out_specs=pl.BlockSpec((1, T, D_MODEL), lambda b, t: (b, t, 0)),
        out_shape=jax.ShapeDtypeStruct((B, S, D_MODEL), F32),
        compiler_params=_params(),
        name="merge",
    )(xs, mod, nw, w_gate, r, s, f, a, w_branch, w_out, fnw)


def _rope_table(S):
    rows = S // GRID_W
    row = jnp.repeat(jnp.arange(rows), GRID_W).astype(F32)
    col = jnp.tile(jnp.arange(GRID_W), rows).astype(F32)
    inv = ROPE_BASE ** (-jnp.arange(ROPE_AXIS_FREQS, dtype=F32) / ROPE_AXIS_FREQS)
    ang = jnp.concatenate([row[:, None] * inv[None], col[:, None] * inv[None]], axis=-1)
    cos, sin = jnp.cos(ang), jnp.sin(ang)
    zero = jnp.zeros_like(sin)
    reps = LANES // ATT_DH
    cos_t = jnp.tile(jnp.concatenate([cos, cos], axis=-1), (1, reps))
    sin_lo = jnp.tile(jnp.concatenate([-sin, zero], axis=-1), (1, reps))
    sin_hi = jnp.tile(jnp.concatenate([zero, sin], axis=-1), (1, reps))
    return jnp.stack([cos_t, sin_lo, sin_hi])


def kernel(x, c, ctx, c_ctx, w_mod, b_mod, norm_w, w_in, ret_decay, ret_gn_w, sc_conv_w, cf_conv_w,
           cf_conv_b, cf_ln_w, cf_ln_b, att_sink, w_branch, w_out, final_norm_w):
    B, S, D = x.shape
    L = ctx.shape[1]
    T = TILE_X

    w_ret = w_in[:, :, RET_COLS[0]:RET_COLS[1]].astype(BF16)
    w_conv = w_in[:, :, CONV_COLS[0]:CONV_COLS[1]].astype(BF16)
    w_att = w_in[:, :, ATT_COLS[0]:ATT_COLS[1]].astype(BF16)
    w_gate = w_in[:, :, GATE_COLS[0]:GATE_COLS[1]].astype(BF16)
    w_br = w_branch.astype(BF16)
    w_o = w_out.astype(BF16)
    rope_tab = _rope_table(S)

    mod_rows = 16
    cc = jnp.concatenate([c, c_ctx[None], jnp.zeros((mod_rows - B - 1, D), F32)], axis=0)
    mod_all = _modulation(cc, w_mod, b_mod)
    mod_x = mod_all[:, :B].reshape(DEPTH, B, 3, D)
    mod_c = jnp.broadcast_to(mod_all[:, B:B + 1].reshape(DEPTH, 1, 3, D), (DEPTH, B, 3, D))

    zero_state = jnp.zeros((B, RET_HEADS, RET_DH, RET_DH), F32)
    xc = ctx
    for l in range(DEPTH):
        last = l == DEPTH - 1
        nw = norm_w[l][None]
        dec = jnp.broadcast_to(ret_decay[l].reshape(2 * RET_HEADS, 1), (2 * RET_HEADS, LANES))
        gn = ret_gn_w[l][None]
        vecs = (sc_conv_w[l], cf_conv_w[l], cf_conv_b[l][None], cf_ln_w[l][None], cf_ln_b[l][None])
        fnw = final_norm_w[None]

        r_c, s_f, s_b = _retention(xc, mod_c[l], nw, w_ret, l, dec, gn, zero_state, zero_state, L)
        a_c, k_c, v_c = _attention_ctx(xc, mod_c[l], nw, w_att, l, att_sink[l])

        r_x, _, _ = _retention(x, mod_x[l], nw, w_ret, l, dec, gn, s_f, s_b, T)
        a_x = _attention_local(x, mod_x[l], nw, w_att, l, rope_tab, k_c, v_c, att_sink[l], T)
        s_x, f_x = _conv_branches(x, mod_x[l], nw, w_conv, l, *vecs, T)
        x = _merge(x, mod_x[l], nw, w_gate, l, r_x, s_x, f_x, a_x, w_br, w_o, fnw, T, last)

        if not last:
            s_c, f_c = _conv_branches(xc, mod_c[l], nw, w_conv, l, *vecs, L)
            xc = _merge(xc, mod_c[l], nw, w_gate, l, r_c, s_c, f_c, a_c, w_br, w_o, fnw, L, False)
    return x
```

```python
import functools

import numpy as np
import jax
import jax.numpy as jnp
from jax import lax
from jax.experimental import pallas as pl
from jax.experimental.pallas import tpu as pltpu

D_MODEL = 1024
DEPTH = 4
GRID_W = 64
BRANCH_W = D_MODEL // 2
RET_HEADS = 4
RET_DH = BRANCH_W // RET_HEADS
RET_CHUNK = 128
SC_KERNEL = 3
CF_KERNEL = 31
ATT_HEADS = 8
ATT_KV_HEADS = 2
ATT_GROUP = ATT_HEADS // ATT_KV_HEADS
ATT_DH = BRANCH_W // ATT_HEADS
ATT_KV_W = ATT_KV_HEADS * ATT_DH
ATT_BLOCK = 128
WINDOW = 128
ROPE_BASE = 10000.0
ROPE_AXIS_FREQS = ATT_DH // 4
EPS = 1e-6
NEG_INF = -1e30
LOG2E = 1.4426950408889634

RET_COLS = (0, 4 * BRANCH_W)
CONV_COLS = (4 * BRANCH_W, 11 * BRANCH_W)
ATT_COLS = (11 * BRANCH_W, 13 * BRANCH_W + 2 * ATT_KV_W)
GATE_COLS = (13 * BRANCH_W + 2 * ATT_KV_W, 13 * BRANCH_W + 2 * ATT_KV_W + 4 * D_MODEL)

LANES = 128
SUBLANES = 8
CONV_HALO = 16
ATT_STACK = 4
CONV_ROW_BLOCK = 64
TILE_X = 512
TILE_CM = 256
VMEM_LIMIT = 56 * 1024 * 1024

F32 = jnp.float32
BF16 = jnp.bfloat16


def _dot(a, b):
    return jnp.dot(a, b, preferred_element_type=F32)


def _dot_nt(a, b):
    return lax.dot_general(a, b, (((1,), (1,)), ((), ())), preferred_element_type=F32)


def _dot_tn(a, b):
    return lax.dot_general(a, b, (((0,), (0,)), ((), ())), preferred_element_type=F32)


def _sigmoid(v):
    return jax.nn.sigmoid(v)


def _silu(v):
    return v * _sigmoid(v)


def _normed(x, mod_ref, nw_ref):
    a = nw_ref[...] * (1.0 + mod_ref[0, 1:2, :])
    shift = mod_ref[0, 0:1, :]
    ms = jnp.mean(x * x, axis=-1, keepdims=True)
    return (x * lax.rsqrt(ms + EPS) * a + shift).astype(BF16)


def _params(**flags):
    return pltpu.CompilerParams(vmem_limit_bytes=VMEM_LIMIT, flags=flags or None)


def _mod_kernel(c_ref, w_ref, b_ref, o_ref):
    cv = c_ref[...]
    o_ref[0] = _dot(_silu(cv).astype(BF16), w_ref[0].astype(BF16)) + b_ref[0]


def _modulation(cc, w_mod, b_mod):
    rows = cc.shape[0]
    tn = 1024
    return pl.pallas_call(
        _mod_kernel,
        grid=(DEPTH, 3 * D_MODEL // tn),
        in_specs=[pl.BlockSpec((rows, D_MODEL), lambda l, n: (0, 0)),
                  pl.BlockSpec((1, D_MODEL, tn), lambda l, n: (l, 0, n)),
                  pl.BlockSpec((1, 1, tn), lambda l, n: (l, 0, n))],
        out_specs=pl.BlockSpec((1, rows, tn), lambda l, n: (l, 0, n)),
        out_shape=jax.ShapeDtypeStruct((DEPTH, rows, 3 * D_MODEL), F32),
        compiler_params=_params(),
        name="modulation",
    )(cc, w_mod, b_mod.reshape(DEPTH, 1, 3 * D_MODEL))


_W_GROUPS = (RET_COLS, CONV_COLS, ATT_COLS, GATE_COLS)


def _split_kernel(w_ref, *outs):
    for (c0, c1), o in zip(_W_GROUPS, outs):
        o[0] = w_ref[0, :, c0:c1].astype(BF16)


def _split_projection(w_in):
    rows = 256
    return pl.pallas_call(
        _split_kernel,
        grid=(DEPTH, D_MODEL // rows),
        in_specs=[pl.BlockSpec((1, rows, w_in.shape[-1]), lambda l, r: (l, r, 0))],
        out_specs=[pl.BlockSpec((1, rows, c1 - c0), lambda l, r: (l, r, 0)) for c0, c1 in _W_GROUPS],
        out_shape=[jax.ShapeDtypeStruct((DEPTH, D_MODEL, c1 - c0), BF16) for c0, c1 in _W_GROUPS],
        compiler_params=_params(),
        name="split_projection",
    )(w_in)


_T_DEC, _T_QF, _T_KF, _T_QB, _T_KB, _T_CF, _T_CB = range(7)


def _ret_kernel(T, S, x_ref, mod_ref, nw_ref, w_ref, dec_ref, gn_ref, s0f_ref, s0b_ref,
                r_ref, sf_ref, sb_ref, qkv, zbuf, obuf, st_f, st_b, tab):
    t = pl.program_id(1)
    nT = S // T
    C = RET_CHUNK
    H = RET_HEADS
    k_scale = RET_DH ** -0.5

    @pl.when(t == 0)
    def _init():
        raw = dec_ref[...]
        lg = jnp.minimum(raw, 0.0) - jnp.log(1.0 + jnp.exp(-jnp.abs(raw)))
        ii = lax.broadcasted_iota(jnp.int32, (C, C), 0)
        jj = lax.broadcasted_iota(jnp.int32, (C, C), 1)
        diff = (ii - jj).astype(F32)
        pos = ii.astype(F32)
        for h in range(H):
            lf = lg[h:h + 1, :]
            lb = lg[H + h:H + h + 1, :]
            dec_f = jnp.where(diff >= 0, jnp.exp(jnp.where(diff >= 0, diff, 0.0) * lf), 0.0)
            dec_b = jnp.where(diff < 0, jnp.exp(jnp.where(diff < 0, -diff, 0.0) * lb), 0.0)
            tab[_T_DEC * H + h] = (dec_f + dec_b) * k_scale
            tab[_T_QF * H + h] = jnp.exp((pos + 1.0) * lf)
            tab[_T_KF * H + h] = jnp.exp((C - 1.0 - pos) * lf) * k_scale
            tab[_T_QB * H + h] = jnp.exp((C - pos) * lb)
            tab[_T_KB * H + h] = jnp.exp(pos * lb) * k_scale
            tab[_T_CF * H + h] = jnp.exp(float(C) * lf) + jnp.zeros((C, LANES), F32)
            tab[_T_CB * H + h] = jnp.exp(float(C) * lb) + jnp.zeros((C, LANES), F32)
        st_f[...] = s0f_ref[0]
        st_b[...] = s0b_ref[0]

    def head_cols(part, h):
        return slice(part * BRANCH_W + h * RET_DH, part * BRANCH_W + (h + 1) * RET_DH)

    @pl.when(t < nT)
    def _forward():
        h_tile = _normed(x_ref[0], mod_ref, nw_ref)
        proj = _dot(h_tile, w_ref[0])
        row0 = pl.multiple_of(t * T, T)
        qkv[pl.ds(row0, T), :] = proj[:, :3 * BRANCH_W].astype(BF16)
        zbuf[pl.ds(row0, T), :] = proj[:, 3 * BRANCH_W:]
        for c in range(T // C):
            rows = pl.ds(pl.multiple_of(t * T + c * C, C), C)
            for h in range(H):
                q = qkv[rows, head_cols(0, h)]
                k = qkv[rows, head_cols(1, h)]
                v = qkv[rows, head_cols(2, h)]
                sc = (_dot_nt(q, k) * tab[_T_DEC * H + h]).astype(BF16)
                qs = (q.astype(F32) * tab[_T_QF * H + h]).astype(BF16)
                kwt = jnp.transpose(k.astype(F32) * tab[_T_KF * H + h]).astype(BF16)
                lhs = jnp.concatenate([jnp.concatenate([sc, qs], axis=1),
                                       jnp.concatenate([kwt, jnp.zeros((RET_DH, RET_DH), BF16)], axis=1)], axis=0)
                res = _dot(lhs, jnp.concatenate([v, st_f[h].astype(BF16)], axis=0))
                obuf[rows, h * RET_DH:(h + 1) * RET_DH] = res[:C]
                st_f[h] = st_f[h] * tab[_T_CF * H + h] + res[C:]

        @pl.when(t == nT - 1)
        def _():
            sf_ref[0] = st_f[...]

    @pl.when(t == nT)
    def _backward():
        def chunk(i, carry):
            rows = pl.ds(pl.multiple_of((S // C - 1 - i) * C, C), C)
            for h in range(H):
                hc = slice(h * RET_DH, (h + 1) * RET_DH)
                q = qkv[rows, head_cols(0, h)]
                k = qkv[rows, head_cols(1, h)]
                v = qkv[rows, head_cols(2, h)]
                qs = (q.astype(F32) * tab[_T_QB * H + h]).astype(BF16)
                kwt = jnp.transpose(k.astype(F32) * tab[_T_KB * H + h]).astype(BF16)
                zero = jnp.zeros((C, RET_DH), BF16)
                lhs = jnp.concatenate([jnp.concatenate([qs, zero], axis=1),
                                       jnp.concatenate([zero, kwt], axis=1)], axis=0)
                res = _dot(lhs, jnp.concatenate([st_b[h].astype(BF16), v], axis=0))
                o = obuf[rows, hc] + res[:C]
                mu = jnp.mean(o, axis=-1, keepdims=True)
                d = o - mu
                var = jnp.mean(d * d, axis=-1, keepdims=True)
                y = d * lax.rsqrt(var + EPS) * gn_ref[:, hc]
                r_ref[0, rows, hc] = (y * _silu(zbuf[rows, hc])).astype(BF16)
                st_b[h] = st_b[h] * tab[_T_CB * H + h] + res[C:]
            return carry

        lax.fori_loop(0, S // C, chunk, 0, unroll=min(4, S // C))
        sb_ref[0] = st_b[...]


def _retention(xs, mod, nw, w_ret, l, dec, gn_w, s0f, s0b, T):
    B, S, _ = xs.shape
    nT = S // T
    st_shape = (RET_HEADS, RET_DH, RET_DH)
    wcols = RET_COLS[1] - RET_COLS[0]
    return pl.pallas_call(
        functools.partial(_ret_kernel, T, S),
        grid=(B, nT + 1),
        in_specs=[
            pl.BlockSpec((1, T, D_MODEL), lambda b, t: (b, jnp.minimum(t, nT - 1), 0)),
            pl.BlockSpec((1, 3, D_MODEL), lambda b, t: (b, 0, 0)),
            pl.BlockSpec((1, D_MODEL), lambda b, t: (0, 0)),
            pl.BlockSpec((1, D_MODEL, wcols), lambda b, t: (l, 0, 0), pipeline_mode=pl.Buffered(1)),
            pl.BlockSpec((2 * RET_HEADS, LANES), lambda b, t: (0, 0)),
            pl.BlockSpec((1, BRANCH_W), lambda b, t: (0, 0)),
            pl.BlockSpec((1,) + st_shape, lambda b, t: (b, 0, 0, 0)),
            pl.BlockSpec((1,) + st_shape, lambda b, t: (b, 0, 0, 0)),
        ],
        out_specs=[
            pl.BlockSpec((1, S, BRANCH_W), lambda b, t: (b, 0, 0)),
            pl.BlockSpec((1,) + st_shape, lambda b, t: (b, 0, 0, 0)),
            pl.BlockSpec((1,) + st_shape, lambda b, t: (b, 0, 0, 0)),
        ],
        out_shape=[
            jax.ShapeDtypeStruct((B, S, BRANCH_W), BF16),
            jax.ShapeDtypeStruct((B,) + st_shape, F32),
            jax.ShapeDtypeStruct((B,) + st_shape, F32),
        ],
        scratch_shapes=[
            pltpu.VMEM((S, 3 * BRANCH_W), BF16),
            pltpu.VMEM((S, BRANCH_W), F32),
            pltpu.VMEM((S, BRANCH_W), F32),
            pltpu.VMEM(st_shape, F32),
            pltpu.VMEM(st_shape, F32),
            pltpu.VMEM((7 * RET_HEADS, RET_CHUNK, LANES), F32),
        ],
        compiler_params=_params(),
        name="retention",
    )(xs, mod, nw, w_ret, dec, gn_w, s0f, s0b)


def _rope(val, rope_ref, r0, nrows):
    cos = rope_ref[0, pl.ds(r0, nrows), :]
    sin_lo = rope_ref[1, pl.ds(r0, nrows), :]
    sin_hi = rope_ref[2, pl.ds(r0, nrows), :]
    outs = []
    for g in range(val.shape[1] // LANES):
        vg = val[:, g * LANES:(g + 1) * LANES]
        outs.append(vg * cos + pltpu.roll(vg, LANES - ATT_DH // 2, 1) * sin_lo
                    + pltpu.roll(vg, ATT_DH // 2, 1) * sin_hi)
    return outs[0] if len(outs) == 1 else jnp.concatenate(outs, axis=1)


def _softmax_av(qh, parts, sink):
    scores = []
    m = None
    for k, _, mask in parts:
        s = _dot_nt(qh, k)
        if mask is not None:
            s = jnp.where(mask, s, NEG_INF)
        scores.append(s)
        sm = jnp.max(s, axis=-1, keepdims=True)
        m = sm if m is None else jnp.maximum(m, sm)
    m = jnp.maximum(m, sink)
    den = jnp.exp(sink - m)
    o = None
    for s, (_, v, _) in zip(scores, parts):
        pr = jnp.exp(s - m)
        den = den + jnp.sum(pr, axis=-1, keepdims=True)
        ov = _dot(pr.astype(BF16), v)
        o = ov if o is None else o + ov
    return o / den


def _v_ext_t(v, hk):
    lane = lax.broadcasted_iota(jnp.int32, v.shape, 1)
    vh = v if hk == 0 else pltpu.roll(v, ATT_DH, 1)
    return jnp.transpose(jnp.where(lane < ATT_DH, vh, 1.0)).astype(BF16)


def _att_local_kernel(T, S, x_ref, xp_ref, xn_ref, mod_ref, nw_ref, w_ref, rope_ref, kc_ref, vct_ref,
                      sink_ref, a_ref, q_s, k_s, vt_s, z_s, ot_s):
    t = pl.program_id(1)
    nb = T // ATT_BLOCK
    blk = ATT_BLOCK
    G = ATT_GROUP
    q_cols = slice(0, BRANCH_W)
    kv_cols = slice(BRANCH_W, BRANCH_W + 2 * ATT_KV_W)
    z_cols = slice(BRANCH_W + 2 * ATT_KV_W, 2 * BRANCH_W + 2 * ATT_KV_W)

    def put_kv(kv, r0, dst):
        k = _rope(kv[:, :ATT_KV_W], rope_ref, r0, kv.shape[0])
        for hk in range(ATT_KV_HEADS):
            k_s[hk, dst, :] = k[:, hk * ATT_DH:(hk + 1) * ATT_DH].astype(BF16)
            vt_s[hk, :, dst] = _v_ext_t(kv[:, ATT_KV_W:], hk)

    hm = _normed(x_ref[0], mod_ref, nw_ref)
    row0 = pl.multiple_of(t * T, T)
    q = _rope(_dot(hm, w_ref[0, :, q_cols]), rope_ref, row0, T) * (ATT_DH ** -0.5 * LOG2E)
    for j in range(nb):
        for hd in range(ATT_HEADS):
            hk, g = divmod(hd, G)
            q_s[j * ATT_KV_HEADS + hk, g * blk:(g + 1) * blk, :] = (
                q[j * blk:(j + 1) * blk, hd * ATT_DH:(hd + 1) * ATT_DH].astype(BF16))
    put_kv(_dot(hm, w_ref[0, :, kv_cols]), row0, slice(blk, blk + T))
    z_s[...] = _silu(_dot(hm, w_ref[0, :, z_cols]))

    rp = pl.multiple_of(jnp.maximum(t * T - blk, 0), blk)
    put_kv(_dot(_normed(xp_ref[0], mod_ref, nw_ref), w_ref[0, :, kv_cols]), rp, slice(0, blk))
    rn = pl.multiple_of(jnp.minimum((t + 1) * T, S - blk), blk)
    put_kv(_dot(_normed(xn_ref[0], mod_ref, nw_ref), w_ref[0, :, kv_cols]), rn, slice(blk + T, 2 * blk + T))

    GS = ATT_STACK
    NQ = GS * blk
    kk = lax.broadcasted_iota(jnp.int32, (blk, NQ), 0)
    qq = lax.broadcasted_iota(jnp.int32, (blk, NQ), 1) & (blk - 1)
    grp = lax.broadcasted_iota(jnp.int32, (1, NQ), 1) // blk
    for hk in range(ATT_KV_HEADS):
        for sub in range(G // GS):
            hd0 = hk * G + sub * GS
            sink = jnp.full((1, NQ), sink_ref[hd0], F32)
            for g in range(1, GS):
                sink = jnp.where(grp == g, sink_ref[hd0 + g], sink)
            sink = sink * LOG2E
            for j in range(nb):
                gb = t * nb + j
                wrows = slice(j * blk, (j + 3) * blk)
                qg = q_s[j * ATT_KV_HEADS + hk, sub * NQ:(sub + 1) * NQ, :]
                s_loc = _dot_nt(k_s[hk, wrows, :], qg)
                s_prev = jnp.where((kk >= qq) & (gb > 0), s_loc[:blk], NEG_INF)
                s_cur = s_loc[blk:2 * blk]
                s_next = jnp.where((kk <= qq) & (gb < S // blk - 1), s_loc[2 * blk:], NEG_INF)
                s_ctx = _dot_nt(kc_ref[0, hk], qg)
                m = jnp.maximum(jnp.maximum(s_prev, s_cur), s_next)
                for cb in range(s_ctx.shape[0] // blk):
                    m = jnp.maximum(m, s_ctx[cb * blk:(cb + 1) * blk])
                m = jnp.maximum(jnp.max(m, axis=0, keepdims=True), sink)
                p_loc = jnp.concatenate([jnp.exp2(s_prev - m), jnp.exp2(s_cur - m), jnp.exp2(s_next - m)], axis=0)
                o = (_dot(vt_s[hk, :, wrows], p_loc.astype(BF16))
                     + _dot(vct_ref[0, hk], jnp.exp2(s_ctx - m).astype(BF16)))
                res = o[:ATT_DH] / (o[ATT_DH:] + jnp.exp2(sink - m))
                for g in range(GS):
                    hd = hd0 + g
                    ot_s[hd * ATT_DH:(hd + 1) * ATT_DH, j * blk:(j + 1) * blk] = res[:, g * blk:(g + 1) * blk]
    a_ref[0] = (jnp.transpose(ot_s[...]) * z_s[...]).astype(BF16)


def _att_ctx_kernel(x_ref, mod_ref, nw_ref, w_ref, sink_ref, a_ref, kc_ref, vct_ref, q_s, z_s):
    q_cols = slice(0, BRANCH_W)
    kv_cols = slice(BRANCH_W, BRANCH_W + 2 * ATT_KV_W)
    z_cols = slice(BRANCH_W + 2 * ATT_KV_W, 2 * BRANCH_W + 2 * ATT_KV_W)
    hm = _normed(x_ref[0], mod_ref, nw_ref)
    q_s[...] = (_dot(hm, w_ref[0, :, q_cols]) * (ATT_DH ** -0.5)).astype(BF16)
    kv = _dot(hm, w_ref[0, :, kv_cols])
    for hk in range(ATT_KV_HEADS):
        kc_ref[0, hk] = kv[:, hk * ATT_DH:(hk + 1) * ATT_DH].astype(BF16)
        vct_ref[0, hk] = _v_ext_t(kv[:, ATT_KV_W:], hk)
    z_s[...] = _silu(_dot(hm, w_ref[0, :, z_cols]))
    for hk in range(ATT_KV_HEADS):
        v_hk = kv[:, ATT_KV_W + hk * ATT_DH:ATT_KV_W + (hk + 1) * ATT_DH].astype(BF16)
        parts = [(kc_ref[0, hk], v_hk, None)]
        for g in range(ATT_GROUP):
            hd = hk * ATT_GROUP + g
            hc = slice(hd * ATT_DH, (hd + 1) * ATT_DH)
            o = _softmax_av(q_s[:, hc], parts, sink_ref[hd])
            a_ref[0, :, hc] = (o * z_s[:, hc]).astype(BF16)


def _attention_local(xs, mod, nw, w_att, l, rope_tab, kc, vc, sink, T):
    B, S, _ = xs.shape
    nT = S // T
    bpt = T // ATT_BLOCK
    nblk = S // ATT_BLOCK
    L = kc.shape[2]
    wcols = ATT_COLS[1] - ATT_COLS[0]
    return pl.pallas_call(
        functools.partial(_att_local_kernel, T, S),
        grid=(B, nT),
        in_specs=[
            pl.BlockSpec((1, T, D_MODEL), lambda b, t: (b, t, 0)),
            pl.BlockSpec((1, ATT_BLOCK, D_MODEL), lambda b, t: (b, jnp.maximum(t * bpt - 1, 0), 0)),
            pl.BlockSpec((1, ATT_BLOCK, D_MODEL), lambda b, t: (b, jnp.minimum((t + 1) * bpt, nblk - 1), 0)),
            pl.BlockSpec((1, 3, D_MODEL), lambda b, t: (b, 0, 0)),
            pl.BlockSpec((1, D_MODEL), lambda b, t: (0, 0)),
            pl.BlockSpec((1, D_MODEL, wcols), lambda b, t: (l, 0, 0)),
            pl.BlockSpec((3, S, LANES), lambda b, t: (0, 0, 0)),
            pl.BlockSpec((1, ATT_KV_HEADS, L, ATT_DH), lambda b, t: (b, 0, 0, 0)),
            pl.BlockSpec((1, ATT_KV_HEADS, 2 * ATT_DH, L), lambda b, t: (b, 0, 0, 0)),
            pl.BlockSpec(memory_space=pltpu.SMEM),
        ],
        out_specs=pl.BlockSpec((1, T, BRANCH_W), lambda b, t: (b, t, 0)),
        out_shape=jax.ShapeDtypeStruct((B, S, BRANCH_W), BF16),
        scratch_shapes=[
            pltpu.VMEM((bpt * ATT_KV_HEADS, ATT_GROUP * ATT_BLOCK, ATT_DH), BF16),
            pltpu.VMEM((ATT_KV_HEADS, T + 2 * ATT_BLOCK, ATT_DH), BF16),
            pltpu.VMEM((ATT_KV_HEADS, 2 * ATT_DH, T + 2 * ATT_BLOCK), BF16),
            pltpu.VMEM((T, BRANCH_W), F32),
            pltpu.VMEM((BRANCH_W, T), F32),
        ],
        compiler_params=_params(),
        name="attention_local",
    )(xs, xs, xs, mod, nw, w_att, rope_tab, kc, vc, sink)


def _attention_ctx(xc, mod, nw, w_att, l, sink):
    B, L, _ = xc.shape
    wcols = ATT_COLS[1] - ATT_COLS[0]
    return pl.pallas_call(
        _att_ctx_kernel,
        grid=(B,),
        in_specs=[
            pl.BlockSpec((1, L, D_MODEL), lambda b: (b, 0, 0)),
            pl.BlockSpec((1, 3, D_MODEL), lambda b: (b, 0, 0)),
            pl.BlockSpec((1, D_MODEL), lambda b: (0, 0)),
            pl.BlockSpec((1, D_MODEL, wcols), lambda b: (l, 0, 0)),
            pl.BlockSpec(memory_space=pltpu.SMEM),
        ],
        out_specs=[
            pl.BlockSpec((1, L, BRANCH_W), lambda b: (b, 0, 0)),
            pl.BlockSpec((1, ATT_KV_HEADS, L, ATT_DH), lambda b: (b, 0, 0, 0)),
            pl.BlockSpec((1, ATT_KV_HEADS, 2 * ATT_DH, L), lambda b: (b, 0, 0, 0)),
        ],
        out_shape=[
            jax.ShapeDtypeStruct((B, L, BRANCH_W), BF16),
            jax.ShapeDtypeStruct((B, ATT_KV_HEADS, L, ATT_DH), BF16),
            jax.ShapeDtypeStruct((B, ATT_KV_HEADS, 2 * ATT_DH, L), BF16),
        ],
        scratch_shapes=[
            pltpu.VMEM((L, BRANCH_W), BF16),
            pltpu.VMEM((L, BRANCH_W), F32),
        ],
        compiler_params=_params(),
        name="attention_ctx",
    )(xc, mod, nw, w_att, sink)


def _sc_phases():
    return tuple(sorted({(CONV_HALO - SC_KERNEL // 2 + k) % SUBLANES for k in range(SC_KERNEL)}))


def _conv_kernel(T, S, x_ref, xp_ref, xn_ref, mod_ref, nw_ref, w_ref, scw_ref, cfw_ref, cfb_ref,
                 lnw_ref, lnb_ref, s_ref, f_ref, hbuf, u_sc, u_cf, wrep):
    t = pl.program_id(1)
    nT = S // T
    HL = CONV_HALO
    W = BRANCH_W
    E = T + 2 * HL
    SUB = SUBLANES
    sc_phases = _sc_phases()
    hbuf[0:HL, :] = _normed(xp_ref[0], mod_ref, nw_ref)
    hbuf[HL:HL + T, :] = _normed(x_ref[0], mod_ref, nw_ref)
    hbuf[HL + T:2 * HL + T, :] = _normed(xn_ref[0], mod_ref, nw_ref)
    hext = hbuf[...]
    hm = hbuf[HL:HL + T, :]

    row = lax.broadcasted_iota(jnp.int32, (E, 1), 0)
    in_seq = ((row >= HL) | (t > 0)) & ((row < HL + T) | (t < nT - 1))

    def store_phases(u, dst, phases):
        for i, ph in enumerate(phases):
            dst[i, 0:E - SUB, :] = u[ph:ph + E - SUB, :]

    cx = _dot(hext, w_ref[0, :, W:3 * W])
    store_phases(jnp.where(in_seq, cx[:, :W] * cx[:, W:], 0.0), u_sc, sc_phases)
    glu = _dot(hext, w_ref[0, :, 4 * W:6 * W])
    store_phases(jnp.where(in_seq, glu[:, :W] * _sigmoid(glu[:, W:]), 0.0), u_cf, range(SUB))

    vec_rows = ([scw_ref[k:k + 1, :] for k in range(SC_KERNEL)] + [cfw_ref[k:k + 1, :] for k in range(CF_KERNEL)]
                + [cfb_ref[...], lnw_ref[...], lnb_ref[...]])
    for i, v in enumerate(vec_rows):
        wrep[i] = jnp.broadcast_to(v, (SUB, W))
    i_cfb, i_lnw, i_lnb = (SC_KERNEL + CF_KERNEL + i for i in range(3))

    RB = CONV_ROW_BLOCK

    def rows3(v):
        return v.reshape(RB // SUB, SUB, W)

    GR = 256
    for rb in range(T // RB):
        r = rb * RB
        if r % GR == 0:
            hg = hbuf[HL + r:HL + r + GR, :]
            b_gate = _dot(hg, w_ref[0, :, 0:W])
            z_sc = _dot(hg, w_ref[0, :, 3 * W:4 * W])
            z_cf = _dot(hg, w_ref[0, :, 6 * W:7 * W])
        rg = r % GR
        acc = None
        for k in range(SC_KERNEL):
            a, ph = divmod(HL - SC_KERNEL // 2 + k, SUB)
            term = wrep[k] * rows3(u_sc[sc_phases.index(ph), r + SUB * a:r + SUB * a + RB, :])
            acc = term if acc is None else acc + term
        acc = acc.reshape(RB, W)
        s_ref[0, r:r + RB, :] = (b_gate[rg:rg + RB] * acc * _silu(z_sc[rg:rg + RB])).astype(BF16)

        acc = wrep[i_cfb]
        for k in range(CF_KERNEL):
            a, ph = divmod(HL - CF_KERNEL // 2 + k, SUB)
            acc = acc + wrep[SC_KERNEL + k] * rows3(u_cf[ph, r + SUB * a:r + SUB * a + RB, :])
        acc = acc.reshape(RB, W)
        mu = jnp.mean(acc, axis=-1, keepdims=True)
        d = acc - mu
        var = jnp.mean(d * d, axis=-1, keepdims=True)
        y = (rows3(d * lax.rsqrt(var + EPS)) * wrep[i_lnw] + wrep[i_lnb]).reshape(RB, W)
        f_ref[0, r:r + RB, :] = (_silu(y) * _silu(z_cf[rg:rg + RB])).astype(BF16)


def _conv_branches(xs, mod, nw, w_conv, l, scw, cfw, cfb, lnw, lnb, T):
    B, S, _ = xs.shape
    nT = S // T
    HL = CONV_HALO
    bpt = T // HL
    nblk = S // HL
    wcols = CONV_COLS[1] - CONV_COLS[0]
    vec = lambda: pl.BlockSpec((1, BRANCH_W), lambda b, t: (0, 0))
    return pl.pallas_call(
        functools.partial(_conv_kernel, T, S),
        grid=(B, nT),
        in_specs=[
            pl.BlockSpec((1, T, D_MODEL), lambda b, t: (b, t, 0)),
            pl.BlockSpec((1, HL, D_MODEL), lambda b, t: (b, jnp.maximum(t * bpt - 1, 0), 0)),
            pl.BlockSpec((1, HL, D_MODEL), lambda b, t: (b, jnp.minimum((t + 1) * bpt, nblk - 1), 0)),
            pl.BlockSpec((1, 3, D_MODEL), lambda b, t: (b, 0, 0)),
            pl.BlockSpec((1, D_MODEL), lambda b, t: (0, 0)),
            pl.BlockSpec((1, D_MODEL, wcols), lambda b, t: (l, 0, 0)),
            pl.BlockSpec((SC_KERNEL, BRANCH_W), lambda b, t: (0, 0)),
            pl.BlockSpec((CF_KERNEL, BRANCH_W), lambda b, t: (0, 0)),
            vec(), vec(), vec(),
        ],
        out_specs=[pl.BlockSpec((1, T, BRANCH_W), lambda b, t: (b, t, 0)),
                   pl.BlockSpec((1, T, BRANCH_W), lambda b, t: (b, t, 0))],
        out_shape=[jax.ShapeDtypeStruct((B, S, BRANCH_W), BF16),
                   jax.ShapeDtypeStruct((B, S, BRANCH_W), BF16)],
        scratch_shapes=[
            pltpu.VMEM((T + 2 * HL, D_MODEL), BF16),
            pltpu.VMEM((len(_sc_phases()), T + 2 * HL, BRANCH_W), F32),
            pltpu.VMEM((SUBLANES, T + 2 * HL, BRANCH_W), F32),
            pltpu.VMEM((SC_KERNEL + CF_KERNEL + 3, SUBLANES, BRANCH_W), F32),
        ],
        compiler_params=_params(),
        name="conv_branches",
    )(xs, xs, xs, mod, nw, w_conv, scw, cfw, cfb, lnw, lnb)


def _merge_kernel(final, x_ref, mod_ref, nw_ref, wg_ref, r_ref, s_ref, f_ref, a_ref, wb_ref, wo_ref,
                  fnw_ref, o_ref):
    x = x_ref[0]
    h = _normed(x, mod_ref, nw_ref)
    merged = None
    for i, br in enumerate((r_ref, s_ref, f_ref, a_ref)):
        g = _dot(h, wg_ref[0, :, i * D_MODEL:(i + 1) * D_MODEL])
        y = _sigmoid(g) * _dot(br[0], wb_ref[0, i])
        merged = y if merged is None else merged + y
    out = _dot(merged.astype(BF16), wo_ref[0])
    xn = x + mod_ref[0, 2:3, :] * out
    if final:
        ms = jnp.mean(xn * xn, axis=-1, keepdims=True)
        xn = xn * lax.rsqrt(ms + EPS) * fnw_ref[...]
    o_ref[0] = xn


def _merge(xs, mod, nw, w_gate, l, r, s, f, a, w_branch, w_out, fnw, T, final):
    B, S, _ = xs.shape
    nT = S // T
    br = lambda: pl.BlockSpec((1, T, BRANCH_W), lambda b, t: (b, t, 0))
    return pl.pallas_call(
        functools.partial(_merge_kernel, final),
        grid=(B, nT),
        in_specs=[
            pl.BlockSpec((1, T, D_MODEL), lambda b, t: (b, t, 0)),
            pl.BlockSpec((1, 3, D_MODEL), lambda b, t: (b, 0, 0)),
            pl.BlockSpec((1, D_MODEL), lambda b, t: (0, 0)),
            pl.BlockSpec((1, D_MODEL, 4 * D_MODEL), lambda b, t: (l, 0, 0)),
            br(), br(), br(), br(),
            pl.BlockSpec((1, 4, BRANCH_W, D_MODEL), lambda b, t: (l, 0, 0, 0)),
            pl.BlockSpec((1, D_MODEL, D_MODEL), lambda b, t: (l, 0, 0)),
            pl.BlockSpec((1, D_MODEL), lambda b, t: (0, 0)),
        ],
        out_specs=pl.BlockSpec((1, T, D_MODEL), lambda b, t: (b, t, 0)),
        out_shape=jax.ShapeDtypeStruct((B, S, D_MODEL), F32),
        compiler_params=_params(),
        name="merge",
    )(xs, mod, nw, w_gate, r, s, f, a, w_branch, w_out, fnw)


def _conv_merge_kernel(T, S, final, x_ref, xp_ref, xn_ref, mod_ref, nw_ref, wc_ref, scw_ref, cfw_ref, cfb_ref,
                       lnw_ref, lnb_ref, wg_ref, r_ref, a_ref, wb_ref, wo_ref, fnw_ref, o_ref,
                       hbuf, u_sc, u_cf, wrep, gz, s_buf, f_buf, sg):
    t = pl.program_id(1)
    nT = S // T
    HL = CONV_HALO
    W = BRANCH_W
    E = T + 2 * HL
    SUB = SUBLANES
    sc_phases = _sc_phases()
    hbuf[0:HL, :] = _normed(xp_ref[0], mod_ref, nw_ref)
    hbuf[HL:HL + T, :] = _normed(x_ref[0], mod_ref, nw_ref)
    hbuf[HL + T:2 * HL + T, :] = _normed(xn_ref[0], mod_ref, nw_ref)
    hext = hbuf[...]
    hm = hbuf[HL:HL + T, :]

    row = lax.broadcasted_iota(jnp.int32, (E, 1), 0)
    in_seq = ((row >= HL) | (t > 0)) & ((row < HL + T) | (t < nT - 1))

    def store_phases(u, dst, phases):
        for i, ph in enumerate(phases):
            dst[i, 0:E - SUB, :] = u[ph:ph + E - SUB, :]

    cx = _dot(hext, wc_ref[0, :, W:3 * W])
    store_phases(jnp.where(in_seq, cx[:, :W] * cx[:, W:], 0.0), u_sc, sc_phases)
    glu = _dot(hext, wc_ref[0, :, 4 * W:6 * W])
    store_phases(jnp.where(in_seq, glu[:, :W] * _sigmoid(glu[:, W:]), 0.0), u_cf, range(SUB))
    gz[0] = _dot(hm, wc_ref[0, :, 0:W])
    gz[1] = _silu(_dot(hm, wc_ref[0, :, 3 * W:4 * W]))
    gz[2] = _silu(_dot(hm, wc_ref[0, :, 6 * W:7 * W]))

    vec_rows = ([scw_ref[k:k + 1, :] for k in range(SC_KERNEL)] + [cfw_ref[k:k + 1, :] for k in range(CF_KERNEL)]
                + [cfb_ref[...], lnw_ref[...], lnb_ref[...]])
    for i, v in enumerate(vec_rows):
        wrep[i] = jnp.broadcast_to(v, (SUB, W))
    i_cfb, i_lnw, i_lnb = (SC_KERNEL + CF_KERNEL + i for i in range(3))

    RB = CONV_ROW_BLOCK
    n_rb = T // RB
    GW = 4 * D_MODEL // n_rb

    def rows3(v):
        return v.reshape(RB // SUB, SUB, W)

    def row_block(i, carry):
        r = pl.multiple_of(i * RB, RB)
        rows = pl.ds(r, RB)
        acc = None
        for k in range(SC_KERNEL):
            a, ph = divmod(HL - SC_KERNEL // 2 + k, SUB)
            term = wrep[k] * rows3(u_sc[sc_phases.index(ph), pl.ds(r + SUB * a, RB), :])
            acc = term if acc is None else acc + term
        s_buf[0, rows, :] = (gz[0, rows, :] * acc.reshape(RB, W) * gz[1, rows, :]).astype(BF16)

        acc = wrep[i_cfb]
        for k in range(CF_KERNEL):
            a, ph = divmod(HL - CF_KERNEL // 2 + k, SUB)
            acc = acc + wrep[SC_KERNEL + k] * rows3(u_cf[ph, pl.ds(r + SUB * a, RB), :])
        acc = acc.reshape(RB, W)
        mu = jnp.mean(acc, axis=-1, keepdims=True)
        d = acc - mu
        var = jnp.mean(d * d, axis=-1, keepdims=True)
        y = (rows3(d * lax.rsqrt(var + EPS)) * wrep[i_lnw] + wrep[i_lnb]).reshape(RB, W)
        f_buf[0, rows, :] = (_silu(y) * gz[2, rows, :]).astype(BF16)

        cols = pl.ds(pl.multiple_of(i * GW, GW), GW)
        sg[:, cols] = _sigmoid(_dot(hbuf[HL:HL + T, :], wg_ref[0, :, cols]))
        return carry

    lax.fori_loop(0, n_rb, row_block, 0)

    merged = None
    for i, br in enumerate((r_ref, s_buf, f_buf, a_ref)):
        y = sg[:, i * D_MODEL:(i + 1) * D_MODEL] * _dot(br[0], wb_ref[0, i])
        merged = y if merged is None else merged + y
    out = _dot(merged.astype(BF16), wo_ref[0])
    xn = x_ref[0] + mod_ref[0, 2:3, :] * out
    if final:
        ms = jnp.mean(xn * xn, axis=-1, keepdims=True)
        xn = xn * lax.rsqrt(ms + EPS) * fnw_ref[...]
    o_ref[0] = xn


def _conv_merge(xs, mod, nw, w_conv, w_gate, l, scw, cfw, cfb, lnw, lnb, r, a, w_branch, w_out, fnw, T, final):
    B, S, _ = xs.shape
    nT = S // T
    HL = CONV_HALO
    bpt = T // HL
    nblk = S // HL
    wcols = CONV_COLS[1] - CONV_COLS[0]
    once = pl.Buffered(1)
    vec = lambda: pl.BlockSpec((1, BRANCH_W), lambda b, t: (0, 0))
    br = lambda: pl.BlockSpec((1, T, BRANCH_W), lambda b, t: (b, t, 0))
    return pl.pallas_call(
        functools.partial(_conv_merge_kernel, T, S, final),
        grid=(B, nT),
        in_specs=[
            pl.BlockSpec((1, T, D_MODEL), lambda b, t: (b, t, 0)),
            pl.BlockSpec((1, HL, D_MODEL), lambda b, t: (b, jnp.maximum(t * bpt - 1, 0), 0)),
            pl.BlockSpec((1, HL, D_MODEL), lambda b, t: (b, jnp.minimum((t + 1) * bpt, nblk - 1), 0)),
            pl.BlockSpec((1, 3, D_MODEL), lambda b, t: (b, 0, 0)),
            pl.BlockSpec((1, D_MODEL), lambda b, t: (0, 0)),
            pl.BlockSpec((1, D_MODEL, wcols), lambda b, t: (l, 0, 0), pipeline_mode=once),
            pl.BlockSpec((SC_KERNEL, BRANCH_W), lambda b, t: (0, 0)),
            pl.BlockSpec((CF_KERNEL, BRANCH_W), lambda b, t: (0, 0)),
            vec(), vec(), vec(),
            pl.BlockSpec((1, D_MODEL, 4 * D_MODEL), lambda b, t: (l, 0, 0), pipeline_mode=once),
            br(), br(),
            pl.BlockSpec((1, 4, BRANCH_W, D_MODEL), lambda b, t: (l, 0, 0, 0), pipeline_mode=once),
            pl.BlockSpec((1, D_MODEL, D_MODEL), lambda b, t: (l, 0, 0), pipeline_mode=once),
            pl.BlockSpec((1, D_MODEL), lambda b, t: (0, 0)),
        ],
        out_specs=pl.BlockSpec((1, T, D_MODEL), lambda b, t: (b, t, 0)),
        out_shape=jax.ShapeDtypeStruct((B, S, D_MODEL), F32),
        scratch_shapes=[
            pltpu.VMEM((T + 2 * HL, D_MODEL), BF16),
            pltpu.VMEM((len(_sc_phases()), T + 2 * HL, BRANCH_W), F32),
            pltpu.VMEM((SUBLANES, T + 2 * HL, BRANCH_W), F32),
            pltpu.VMEM((SC_KERNEL + CF_KERNEL + 3, SUBLANES, BRANCH_W), F32),
            pltpu.VMEM((3, T, BRANCH_W), F32),
            pltpu.VMEM((1, T, BRANCH_W), BF16),
            pltpu.VMEM((1, T, BRANCH_W), BF16),
            pltpu.VMEM((T, 4 * D_MODEL), F32),
        ],
        compiler_params=_params(),
        name="conv_merge",
    )(xs, xs, xs, mod, nw, w_conv, scw, cfw, cfb, lnw, lnb, w_gate, r, a, w_branch, w_out, fnw)


def _rope_table(S):
    rows = S // GRID_W
    row = jnp.repeat(jnp.arange(rows), GRID_W).astype(F32)
    col = jnp.tile(jnp.arange(GRID_W), rows).astype(F32)
    inv = ROPE_BASE ** (-jnp.arange(ROPE_AXIS_FREQS, dtype=F32) / ROPE_AXIS_FREQS)
    ang = jnp.concatenate([row[:, None] * inv[None], col[:, None] * inv[None]], axis=-1)
    cos, sin = jnp.cos(ang), jnp.sin(ang)
    zero = jnp.zeros_like(sin)
    reps = LANES // ATT_DH
    cos_t = jnp.tile(jnp.concatenate([cos, cos], axis=-1), (1, reps))
    sin_lo = jnp.tile(jnp.concatenate([-sin, zero], axis=-1), (1, reps))
    sin_hi = jnp.tile(jnp.concatenate([zero, sin], axis=-1), (1, reps))
    return jnp.stack([cos_t, sin_lo, sin_hi])


def kernel(x, c, ctx, c_ctx, w_mod, b_mod, norm_w, w_in, ret_decay, ret_gn_w, sc_conv_w, cf_conv_w,
           cf_conv_b, cf_ln_w, cf_ln_b, att_sink, w_branch, w_out, final_norm_w):
    B, S, D = x.shape
    L = ctx.shape[1]
    T = TILE_X

    w_ret, w_conv, w_att, w_gate = _split_projection(w_in)
    w_br = w_branch.astype(BF16)
    w_o = w_out.astype(BF16)
    rope_tab = _rope_table(S)

    mod_rows = 16
    cc = jnp.concatenate([c, c_ctx[None], jnp.zeros((mod_rows - B - 1, D), F32)], axis=0)
    mod_all = _modulation(cc, w_mod, b_mod)
    mod_x = mod_all[:, :B].reshape(DEPTH, B, 3, D)
    mod_c = jnp.broadcast_to(mod_all[:, B:B + 1].reshape(DEPTH, 1, 3, D), (DEPTH, B, 3, D))

    zero_state = jnp.zeros((B, RET_HEADS, RET_DH, RET_DH), F32)
    xc = ctx
    for l in range(DEPTH):
        last = l == DEPTH - 1
        nw = norm_w[l][None]
        dec = jnp.broadcast_to(ret_decay[l].reshape(2 * RET_HEADS, 1), (2 * RET_HEADS, LANES))
        gn = ret_gn_w[l][None]
        vecs = (sc_conv_w[l], cf_conv_w[l], cf_conv_b[l][None], cf_ln_w[l][None], cf_ln_b[l][None])
        fnw = final_norm_w[None]

        r_c, s_f, s_b = _retention(xc, mod_c[l], nw, w_ret, l, dec, gn, zero_state, zero_state, L)
        a_c, k_c, v_c = _attention_ctx(xc, mod_c[l], nw, w_att, l, att_sink[l])

        r_x, _, _ = _retention(x, mod_x[l], nw, w_ret, l, dec, gn, s_f, s_b, T)
        a_x = _attention_local(x, mod_x[l], nw, w_att, l, rope_tab, k_c, v_c, att_sink[l], T)
        s_x, f_x = _conv_branches(x, mod_x[l], nw, w_conv, l, *vecs, T)
        x = _merge(x, mod_x[l], nw, w_gate, l, r_x, s_x, f_x, a_x, w_br, w_o, fnw, T, last)

        if not last:
            s_c, f_c = _conv_branches(xc, mod_c[l], nw, w_conv, l, *vecs, L)
            xc = _merge(xc, mod_c[l], nw, w_gate, l, r_c, s_c, f_c, a_c, w_br, w_o, fnw, L, False)
    return x
```

```python
import functools

import jax
import jax.numpy as jnp
from jax import lax
from jax.experimental import pallas as pl
from jax.experimental.pallas import tpu as pltpu

D_MODEL = 1024
DEPTH = 4
GRID_W = 64
BRANCH_W = D_MODEL // 2
RET_HEADS = 4
RET_DH = BRANCH_W // RET_HEADS
RET_CHUNK = 128
SC_KERNEL = 3
CF_KERNEL = 31
ATT_HEADS = 8
ATT_KV_HEADS = 2
ATT_GROUP = ATT_HEADS // ATT_KV_HEADS
ATT_DH = BRANCH_W // ATT_HEADS
ATT_KV_W = ATT_KV_HEADS * ATT_DH
ATT_BLOCK = 128
WINDOW = 128
ROPE_BASE = 10000.0
ROPE_AXIS_FREQS = ATT_DH // 4
EPS = 1e-6
NEG_INF = -1e30
LOG2E = 1.4426950408889634

RET_COLS = (0, 4 * BRANCH_W)
CONV_COLS = (4 * BRANCH_W, 11 * BRANCH_W)
ATT_COLS = (11 * BRANCH_W, 13 * BRANCH_W + 2 * ATT_KV_W)
GATE_COLS = (13 * BRANCH_W + 2 * ATT_KV_W, 13 * BRANCH_W + 2 * ATT_KV_W + 4 * D_MODEL)

LANES = 128
SUBLANES = 8
CONV_HALO = 16
CONV_ROW_BLOCK = 64
TILE_X = 512
TILE_ATT = 1024
TILE_MERGE = 1024
MERGE_SUB_ROWS = 512
VMEM_LIMIT = 56 * 1024 * 1024

F32 = jnp.float32
BF16 = jnp.bfloat16


def _dot(a, b):
    return jnp.dot(a, b, preferred_element_type=F32)


def _dot_nt(a, b):
    return lax.dot_general(a, b, (((1,), (1,)), ((), ())), preferred_element_type=F32)


def _sigmoid(v):
    return jax.nn.sigmoid(v)


def _silu(v):
    return v * _sigmoid(v)


def _normed(x, mod_ref, nw_ref):
    a = nw_ref[...] * (1.0 + mod_ref[0, 1:2, :])
    shift = mod_ref[0, 0:1, :]
    ms = jnp.mean(x * x, axis=-1, keepdims=True)
    return (x * lax.rsqrt(ms + EPS) * a + shift).astype(BF16)


def _params():
    return pltpu.CompilerParams(vmem_limit_bytes=VMEM_LIMIT)


def _mod_kernel(c_ref, w_ref, b_ref, o_ref):
    cv = c_ref[...]
    o_ref[0] = _dot(_silu(cv).astype(BF16), w_ref[0].astype(BF16)) + b_ref[0]


def _modulation(cc, w_mod, b_mod):
    rows = cc.shape[0]
    tn = 1024
    return pl.pallas_call(
        _mod_kernel,
        grid=(DEPTH, 3 * D_MODEL // tn),
        in_specs=[pl.BlockSpec((rows, D_MODEL), lambda l, n: (0, 0)),
                  pl.BlockSpec((1, D_MODEL, tn), lambda l, n: (l, 0, n)),
                  pl.BlockSpec((1, 1, tn), lambda l, n: (l, 0, n))],
        out_specs=pl.BlockSpec((1, rows, tn), lambda l, n: (l, 0, n)),
        out_shape=jax.ShapeDtypeStruct((DEPTH, rows, 3 * D_MODEL), F32),
        compiler_params=_params(),
        name="modulation",
    )(cc, w_mod, b_mod.reshape(DEPTH, 1, 3 * D_MODEL))


_W_GROUPS = (RET_COLS, CONV_COLS, ATT_COLS, GATE_COLS)


def _split_kernel(w_ref, *outs):
    for (c0, c1), o in zip(_W_GROUPS, outs):
        o[0] = w_ref[0, :, c0:c1].astype(BF16)


def _split_projection(w_in):
    rows = 256
    return pl.pallas_call(
        _split_kernel,
        grid=(DEPTH, D_MODEL // rows),
        in_specs=[pl.BlockSpec((1, rows, w_in.shape[-1]), lambda l, r: (l, r, 0))],
        out_specs=[pl.BlockSpec((1, rows, c1 - c0), lambda l, r: (l, r, 0)) for c0, c1 in _W_GROUPS],
        out_shape=[jax.ShapeDtypeStruct((DEPTH, D_MODEL, c1 - c0), BF16) for c0, c1 in _W_GROUPS],
        compiler_params=_params(),
        name="split_projection",
    )(w_in)


_T_DEC, _T_QF, _T_KF, _T_QB, _T_KB, _T_CF, _T_CB = range(7)


def _ret_kernel(T, S, x_ref, mod_ref, nw_ref, w_ref, dec_ref, gn_ref, s0f_ref, s0b_ref,
                r_ref, sf_ref, sb_ref, qkv, zbuf, obuf, st_f, st_b, tab):
    t = pl.program_id(1)
    nT = S // T
    C = RET_CHUNK
    H = RET_HEADS
    k_scale = RET_DH ** -0.5

    @pl.when(t == 0)
    def _init():
        raw = dec_ref[...]
        lg = jnp.minimum(raw, 0.0) - jnp.log(1.0 + jnp.exp(-jnp.abs(raw)))
        ii = lax.broadcasted_iota(jnp.int32, (C, C), 0)
        jj = lax.broadcasted_iota(jnp.int32, (C, C), 1)
        diff = (ii - jj).astype(F32)
        pos = ii.astype(F32)
        for h in range(H):
            lf = lg[h:h + 1, :]
            lb = lg[H + h:H + h + 1, :]
            dec_f = jnp.where(diff >= 0, jnp.exp(jnp.where(diff >= 0, diff, 0.0) * lf), 0.0)
            dec_b = jnp.where(diff < 0, jnp.exp(jnp.where(diff < 0, -diff, 0.0) * lb), 0.0)
            tab[_T_DEC * H + h] = (dec_f + dec_b) * k_scale
            tab[_T_QF * H + h] = jnp.exp((pos + 1.0) * lf)
            tab[_T_KF * H + h] = jnp.exp((C - 1.0 - pos) * lf) * k_scale
            tab[_T_QB * H + h] = jnp.exp((C - pos) * lb)
            tab[_T_KB * H + h] = jnp.exp(pos * lb) * k_scale
            tab[_T_CF * H + h] = jnp.exp(float(C) * lf) + jnp.zeros((C, LANES), F32)
            tab[_T_CB * H + h] = jnp.exp(float(C) * lb) + jnp.zeros((C, LANES), F32)
        st_f[...] = s0f_ref[0]
        st_b[...] = s0b_ref[0]

    def head_cols(part, h):
        return slice(part * BRANCH_W + h * RET_DH, part * BRANCH_W + (h + 1) * RET_DH)

    @pl.when(t < nT)
    def _forward():
        h_tile = _normed(x_ref[0], mod_ref, nw_ref)
        proj = _dot(h_tile, w_ref[0])
        row0 = pl.multiple_of(t * T, T)
        qkv[pl.ds(row0, T), :] = proj[:, :3 * BRANCH_W].astype(BF16)
        zbuf[pl.ds(row0, T), :] = proj[:, 3 * BRANCH_W:]
        for c in range(T // C):
            rows = pl.ds(pl.multiple_of(t * T + c * C, C), C)
            for h in range(H):
                q = qkv[rows, head_cols(0, h)]
                k = qkv[rows, head_cols(1, h)]
                v = qkv[rows, head_cols(2, h)]
                sc = (_dot_nt(q, k) * tab[_T_DEC * H + h]).astype(BF16)
                qs = (q.astype(F32) * tab[_T_QF * H + h]).astype(BF16)
                kwt = jnp.transpose(k.astype(F32) * tab[_T_KF * H + h]).astype(BF16)
                lhs = jnp.concatenate([jnp.concatenate([sc, qs], axis=1),
                                       jnp.concatenate([kwt, jnp.zeros((RET_DH, RET_DH), BF16)], axis=1)], axis=0)
                res = _dot(lhs, jnp.concatenate([v, st_f[h].astype(BF16)], axis=0))
                obuf[rows, h * RET_DH:(h + 1) * RET_DH] = res[:C]
                st_f[h] = st_f[h] * tab[_T_CF * H + h] + res[C:]

        @pl.when(t == nT - 1)
        def _():
            sf_ref[0] = st_f[...]

    @pl.when(t == nT)
    def _backward():
        def chunk(i, carry):
            rows = pl.ds(pl.multiple_of((S // C - 1 - i) * C, C), C)
            for h in range(H):
                hc = slice(h * RET_DH, (h + 1) * RET_DH)
                q = qkv[rows, head_cols(0, h)]
                k = qkv[rows, head_cols(1, h)]
                v = qkv[rows, head_cols(2, h)]
                qs = (q.astype(F32) * tab[_T_QB * H + h]).astype(BF16)
                kwt = jnp.transpose(k.astype(F32) * tab[_T_KB * H + h]).astype(BF16)
                zero = jnp.zeros((C, RET_DH), BF16)
                lhs = jnp.concatenate([jnp.concatenate([qs, zero], axis=1),
                                       jnp.concatenate([zero, kwt], axis=1)], axis=0)
                res = _dot(lhs, jnp.concatenate([st_b[h].astype(BF16), v], axis=0))
                o = obuf[rows, hc] + res[:C]
                mu = jnp.mean(o, axis=-1, keepdims=True)
                d = o - mu
                var = jnp.mean(d * d, axis=-1, keepdims=True)
                y = d * lax.rsqrt(var + EPS) * gn_ref[:, hc]
                r_ref[0, rows, hc] = (y * _silu(zbuf[rows, hc])).astype(BF16)
                st_b[h] = st_b[h] * tab[_T_CB * H + h] + res[C:]
            return carry

        lax.fori_loop(0, S // C, chunk, 0, unroll=min(4, S // C))
        sb_ref[0] = st_b[...]


def _retention(xs, mod, nw, w_ret, l, dec, gn_w, s0f, s0b, T):
    B, S, _ = xs.shape
    nT = S // T
    st_shape = (RET_HEADS, RET_DH, RET_DH)
    wcols = RET_COLS[1] - RET_COLS[0]
    return pl.pallas_call(
        functools.partial(_ret_kernel, T, S),
        grid=(B, nT + 1),
        in_specs=[
            pl.BlockSpec((1, T, D_MODEL), lambda b, t: (b, jnp.minimum(t, nT - 1), 0)),
            pl.BlockSpec((1, 3, D_MODEL), lambda b, t: (b, 0, 0)),
            pl.BlockSpec((1, D_MODEL), lambda b, t: (0, 0)),
            pl.BlockSpec((1, D_MODEL, wcols), lambda b, t: (l, 0, 0), pipeline_mode=pl.Buffered(1)),
            pl.BlockSpec((2 * RET_HEADS, LANES), lambda b, t: (0, 0)),
            pl.BlockSpec((1, BRANCH_W), lambda b, t: (0, 0)),
            pl.BlockSpec((1,) + st_shape, lambda b, t: (b, 0, 0, 0)),
            pl.BlockSpec((1,) + st_shape, lambda b, t: (b, 0, 0, 0)),
        ],
        out_specs=[
            pl.BlockSpec((1, S, BRANCH_W), lambda b, t: (b, 0, 0)),
            pl.BlockSpec((1,) + st_shape, lambda b, t: (b, 0, 0, 0)),
            pl.BlockSpec((1,) + st_shape, lambda b, t: (b, 0, 0, 0)),
        ],
        out_shape=[
            jax.ShapeDtypeStruct((B, S, BRANCH_W), BF16),
            jax.ShapeDtypeStruct((B,) + st_shape, F32),
            jax.ShapeDtypeStruct((B,) + st_shape, F32),
        ],
        scratch_shapes=[
            pltpu.VMEM((S, 3 * BRANCH_W), BF16),
            pltpu.VMEM((S, BRANCH_W), F32),
            pltpu.VMEM((S, BRANCH_W), F32),
            pltpu.VMEM(st_shape, F32),
            pltpu.VMEM(st_shape, F32),
            pltpu.VMEM((7 * RET_HEADS, RET_CHUNK, LANES), F32),
        ],
        compiler_params=_params(),
        name="retention",
    )(xs, mod, nw, w_ret, dec, gn_w, s0f, s0b)


def _rope(val, rope_ref, r0, nrows):
    cos = rope_ref[0, pl.ds(r0, nrows), :]
    sin_lo = rope_ref[1, pl.ds(r0, nrows), :]
    sin_hi = rope_ref[2, pl.ds(r0, nrows), :]
    outs = []
    for g in range(val.shape[1] // LANES):
        vg = val[:, g * LANES:(g + 1) * LANES]
        outs.append(vg * cos + pltpu.roll(vg, LANES - ATT_DH // 2, 1) * sin_lo
                    + pltpu.roll(vg, ATT_DH // 2, 1) * sin_hi)
    return outs[0] if len(outs) == 1 else jnp.concatenate(outs, axis=1)


def _softmax_av(qh, parts, sink):
    scores = []
    m = None
    for k, _, mask in parts:
        s = _dot_nt(qh, k)
        if mask is not None:
            s = jnp.where(mask, s, NEG_INF)
        scores.append(s)
        sm = jnp.max(s, axis=-1, keepdims=True)
        m = sm if m is None else jnp.maximum(m, sm)
    m = jnp.maximum(m, sink)
    den = jnp.exp(sink - m)
    o = None
    for s, (_, v, _) in zip(scores, parts):
        pr = jnp.exp(s - m)
        den = den + jnp.sum(pr, axis=-1, keepdims=True)
        ov = _dot(pr.astype(BF16), v)
        o = ov if o is None else o + ov
    return o / den


def _v_ext_t(v, hk):
    lane = lax.broadcasted_iota(jnp.int32, v.shape, 1)
    vh = v if hk == 0 else pltpu.roll(v, ATT_DH, 1)
    return jnp.transpose(jnp.where(lane < ATT_DH, vh, 1.0)).astype(BF16)


def _att_local_kernel(T, S, x_ref, xp_ref, xn_ref, mod_ref, nw_ref, w_ref, rope_ref, kc_ref, vct_ref,
                      sink_ref, a_ref, q_s, k_s, vt_s, z_s, ot_s):
    t = pl.program_id(1)
    nb = T // ATT_BLOCK
    blk = ATT_BLOCK
    G = ATT_GROUP
    q_cols = slice(0, BRANCH_W)
    kv_cols = slice(BRANCH_W, BRANCH_W + 2 * ATT_KV_W)
    z_cols = slice(BRANCH_W + 2 * ATT_KV_W, 2 * BRANCH_W + 2 * ATT_KV_W)

    def put_kv(kv, r0, dst):
        k = _rope(kv[:, :ATT_KV_W], rope_ref, r0, kv.shape[0])
        for hk in range(ATT_KV_HEADS):
            k_s[hk, dst, :] = k[:, hk * ATT_DH:(hk + 1) * ATT_DH].astype(BF16)
            vt_s[hk, :, dst] = _v_ext_t(kv[:, ATT_KV_W:], hk)

    hm = _normed(x_ref[0], mod_ref, nw_ref)
    row0 = pl.multiple_of(t * T, T)
    q = _rope(_dot(hm, w_ref[0, :, q_cols]), rope_ref, row0, T) * (ATT_DH ** -0.5 * LOG2E)
    for j in range(nb):
        for hd in range(ATT_HEADS):
            hk, g = divmod(hd, G)
            q_s[j * ATT_KV_HEADS + hk, g * blk:(g + 1) * blk, :] = (
                q[j * blk:(j + 1) * blk, hd * ATT_DH:(hd + 1) * ATT_DH].astype(BF16))
    put_kv(_dot(hm, w_ref[0, :, kv_cols]), row0, slice(blk, blk + T))
    z_s[...] = _silu(_dot(hm, w_ref[0, :, z_cols]))

    rp = pl.multiple_of(jnp.maximum(t * T - blk, 0), blk)
    put_kv(_dot(_normed(xp_ref[0], mod_ref, nw_ref), w_ref[0, :, kv_cols]), rp, slice(0, blk))
    rn = pl.multiple_of(jnp.minimum((t + 1) * T, S - blk), blk)
    put_kv(_dot(_normed(xn_ref[0], mod_ref, nw_ref), w_ref[0, :, kv_cols]), rn, slice(blk + T, 2 * blk + T))

    NQ = G * blk
    kk = lax.broadcasted_iota(jnp.int32, (blk, NQ), 0)
    qq = lax.broadcasted_iota(jnp.int32, (blk, NQ), 1) & (blk - 1)
    grp = lax.broadcasted_iota(jnp.int32, (1, NQ), 1) // blk
    sinks = []
    for hk in range(ATT_KV_HEADS):
        sink = jnp.full((1, NQ), sink_ref[hk * G], F32)
        for g in range(1, G):
            sink = jnp.where(grp == g, sink_ref[hk * G + g], sink)
        sinks.append(sink * LOG2E)
    for j in range(nb):
        gb = t * nb + j
        wrows = slice(j * blk, (j + 3) * blk)
        qrows = slice(j * blk, (j + 1) * blk)
        for hk in range(ATT_KV_HEADS):
            q4 = q_s[j * ATT_KV_HEADS + hk]
            s_loc = _dot_nt(k_s[hk, wrows, :], q4)
            s_prev = jnp.where((kk >= qq) & (gb > 0), s_loc[:blk], NEG_INF)
            s_cur = s_loc[blk:2 * blk]
            s_next = jnp.where((kk <= qq) & (gb < S // blk - 1), s_loc[2 * blk:], NEG_INF)
            s_ctx = _dot_nt(kc_ref[0, hk], q4)
            m = jnp.maximum(jnp.maximum(s_prev, s_cur), s_next)
            for cb in range(s_ctx.shape[0] // blk):
                m = jnp.maximum(m, s_ctx[cb * blk:(cb + 1) * blk])
            m = jnp.maximum(jnp.max(m, axis=0, keepdims=True), sinks[hk])
            p_loc = jnp.concatenate([jnp.exp2(s_prev - m), jnp.exp2(s_cur - m), jnp.exp2(s_next - m)], axis=0)
            o = (_dot(vt_s[hk, :, wrows], p_loc.astype(BF16))
                 + _dot(vct_ref[0, hk], jnp.exp2(s_ctx - m).astype(BF16)))
            res = o[:ATT_DH] / (o[ATT_DH:] + jnp.exp2(sinks[hk] - m))
            for g in range(G):
                hd = hk * G + g
                ot_s[j, hd * ATT_DH:(hd + 1) * ATT_DH, :] = res[:, g * blk:(g + 1) * blk]
        a_ref[0, qrows, :] = (jnp.transpose(ot_s[j]) * z_s[qrows, :]).astype(BF16)


def _att_ctx_kernel(x_ref, mod_ref, nw_ref, w_ref, sink_ref, a_ref, kc_ref, vct_ref, q_s, z_s):
    q_cols = slice(0, BRANCH_W)
    kv_cols = slice(BRANCH_W, BRANCH_W + 2 * ATT_KV_W)
    z_cols = slice(BRANCH_W + 2 * ATT_KV_W, 2 * BRANCH_W + 2 * ATT_KV_W)
    hm = _normed(x_ref[0], mod_ref, nw_ref)
    q_s[...] = (_dot(hm, w_ref[0, :, q_cols]) * (ATT_DH ** -0.5)).astype(BF16)
    kv = _dot(hm, w_ref[0, :, kv_cols])
    for hk in range(ATT_KV_HEADS):
        kc_ref[0, hk] = kv[:, hk * ATT_DH:(hk + 1) * ATT_DH].astype(BF16)
        vct_ref[0, hk] = _v_ext_t(kv[:, ATT_KV_W:], hk)
    z_s[...] = _silu(_dot(hm, w_ref[0, :, z_cols]))
    for hk in range(ATT_KV_HEADS):
        v_hk = kv[:, ATT_KV_W + hk * ATT_DH:ATT_KV_W + (hk + 1) * ATT_DH].astype(BF16)
        parts = [(kc_ref[0, hk], v_hk, None)]
        for g in range(ATT_GROUP):
            hd = hk * ATT_GROUP + g
            hc = slice(hd * ATT_DH, (hd + 1) * ATT_DH)
            o = _softmax_av(q_s[:, hc], parts, sink_ref[hd])
            a_ref[0, :, hc] = (o * z_s[:, hc]).astype(BF16)


def _attention_local(xs, mod, nw, w_att, l, rope_tab, kc, vc, sink, T):
    B, S, _ = xs.shape
    nT = S // T
    bpt = T // ATT_BLOCK
    nblk = S // ATT_BLOCK
    L = kc.shape[2]
    wcols = ATT_COLS[1] - ATT_COLS[0]
    return pl.pallas_call(
        functools.partial(_att_local_kernel, T, S),
        grid=(B, nT),
        in_specs=[
            pl.BlockSpec((1, T, D_MODEL), lambda b, t: (b, t, 0)),
            pl.BlockSpec((1, ATT_BLOCK, D_MODEL), lambda b, t: (b, jnp.maximum(t * bpt - 1, 0), 0)),
            pl.BlockSpec((1, ATT_BLOCK, D_MODEL), lambda b, t: (b, jnp.minimum((t + 1) * bpt, nblk - 1), 0)),
            pl.BlockSpec((1, 3, D_MODEL), lambda b, t: (b, 0, 0)),
            pl.BlockSpec((1, D_MODEL), lambda b, t: (0, 0)),
            pl.BlockSpec((1, D_MODEL, wcols), lambda b, t: (l, 0, 0)),
            pl.BlockSpec((3, S, LANES), lambda b, t: (0, 0, 0)),
            pl.BlockSpec((1, ATT_KV_HEADS, L, ATT_DH), lambda b, t: (b, 0, 0, 0)),
            pl.BlockSpec((1, ATT_KV_HEADS, 2 * ATT_DH, L), lambda b, t: (b, 0, 0, 0)),
            pl.BlockSpec(memory_space=pltpu.SMEM),
        ],
        out_specs=pl.BlockSpec((1, T, BRANCH_W), lambda b, t: (b, t, 0)),
        out_shape=jax.ShapeDtypeStruct((B, S, BRANCH_W), BF16),
        scratch_shapes=[
            pltpu.VMEM((bpt * ATT_KV_HEADS, ATT_GROUP * ATT_BLOCK, ATT_DH), BF16),
            pltpu.VMEM((ATT_KV_HEADS, T + 2 * ATT_BLOCK, ATT_DH), BF16),
            pltpu.VMEM((ATT_KV_HEADS, 2 * ATT_DH, T + 2 * ATT_BLOCK), BF16),
            pltpu.VMEM((T, BRANCH_W), F32),
            pltpu.VMEM((bpt, BRANCH_W, ATT_BLOCK), F32),
        ],
        compiler_params=_params(),
        name="attention_local",
    )(xs, xs, xs, mod, nw, w_att, rope_tab, kc, vc, sink)


def _attention_ctx(xc, mod, nw, w_att, l, sink):
    B, L, _ = xc.shape
    wcols = ATT_COLS[1] - ATT_COLS[0]
    return pl.pallas_call(
        _att_ctx_kernel,
        grid=(B,),
        in_specs=[
            pl.BlockSpec((1, L, D_MODEL), lambda b: (b, 0, 0)),
            pl.BlockSpec((1, 3, D_MODEL), lambda b: (b, 0, 0)),
            pl.BlockSpec((1, D_MODEL), lambda b: (0, 0)),
            pl.BlockSpec((1, D_MODEL, wcols), lambda b: (l, 0, 0)),
            pl.BlockSpec(memory_space=pltpu.SMEM),
        ],
        out_specs=[
            pl.BlockSpec((1, L, BRANCH_W), lambda b: (b, 0, 0)),
            pl.BlockSpec((1, ATT_KV_HEADS, L, ATT_DH), lambda b: (b, 0, 0, 0)),
            pl.BlockSpec((1, ATT_KV_HEADS, 2 * ATT_DH, L), lambda b: (b, 0, 0, 0)),
        ],
        out_shape=[
            jax.ShapeDtypeStruct((B, L, BRANCH_W), BF16),
            jax.ShapeDtypeStruct((B, ATT_KV_HEADS, L, ATT_DH), BF16),
            jax.ShapeDtypeStruct((B, ATT_KV_HEADS, 2 * ATT_DH, L), BF16),
        ],
        scratch_shapes=[
            pltpu.VMEM((L, BRANCH_W), BF16),
            pltpu.VMEM((L, BRANCH_W), F32),
        ],
        compiler_params=_params(),
        name="attention_ctx",
    )(xc, mod, nw, w_att, sink)


def _sc_phases():
    return tuple(sorted({(CONV_HALO - SC_KERNEL // 2 + k) % SUBLANES for k in range(SC_KERNEL)}))


def _conv_kernel(T, S, x_ref, xp_ref, xn_ref, mod_ref, nw_ref, w_ref, scw_ref, cfw_ref, cfb_ref,
                 lnw_ref, lnb_ref, s_ref, f_ref, hbuf, u_sc, u_cf, wrep):
    t = pl.program_id(1)
    nT = S // T
    HL = CONV_HALO
    W = BRANCH_W
    E = T + 2 * HL
    SUB = SUBLANES
    sc_phases = _sc_phases()
    hbuf[0:HL, :] = _normed(xp_ref[0], mod_ref, nw_ref)
    hbuf[HL:HL + T, :] = _normed(x_ref[0], mod_ref, nw_ref)
    hbuf[HL + T:2 * HL + T, :] = _normed(xn_ref[0], mod_ref, nw_ref)
    hext = hbuf[...]
    hm = hbuf[HL:HL + T, :]

    row = lax.broadcasted_iota(jnp.int32, (E, 1), 0)
    in_seq = ((row >= HL) | (t > 0)) & ((row < HL + T) | (t < nT - 1))

    def store_phases(u, dst, phases):
        for i, ph in enumerate(phases):
            dst[i, 0:E - SUB, :] = u[ph:ph + E - SUB, :]

    cx = _dot(hext, w_ref[0, :, W:3 * W])
    store_phases(jnp.where(in_seq, cx[:, :W] * cx[:, W:], 0.0), u_sc, sc_phases)
    glu = _dot(hext, w_ref[0, :, 4 * W:6 * W])
    store_phases(jnp.where(in_seq, glu[:, :W] * _sigmoid(glu[:, W:]), 0.0), u_cf, range(SUB))

    vec_rows = ([scw_ref[k:k + 1, :] for k in range(SC_KERNEL)] + [cfw_ref[k:k + 1, :] for k in range(CF_KERNEL)]
                + [cfb_ref[...], lnw_ref[...], lnb_ref[...]])
    for i, v in enumerate(vec_rows):
        wrep[i] = jnp.broadcast_to(v, (SUB, W))
    i_cfb, i_lnw, i_lnb = (SC_KERNEL + CF_KERNEL + i for i in range(3))

    RB = CONV_ROW_BLOCK

    def rows3(v):
        return v.reshape(RB // SUB, SUB, W)

    GR = 256
    for rb in range(T // RB):
        r = rb * RB
        if r % GR == 0:
            hg = hbuf[HL + r:HL + r + GR, :]
            b_gate = _dot(hg, w_ref[0, :, 0:W])
            z_sc = _dot(hg, w_ref[0, :, 3 * W:4 * W])
            z_cf = _dot(hg, w_ref[0, :, 6 * W:7 * W])
        rg = r % GR
        acc = None
        for k in range(SC_KERNEL):
            a, ph = divmod(HL - SC_KERNEL // 2 + k, SUB)
            term = wrep[k] * rows3(u_sc[sc_phases.index(ph), r + SUB * a:r + SUB * a + RB, :])
            acc = term if acc is None else acc + term
        acc = acc.reshape(RB, W)
        s_ref[0, r:r + RB, :] = (b_gate[rg:rg + RB] * acc * _silu(z_sc[rg:rg + RB])).astype(BF16)

        acc = wrep[i_cfb]
        for k in range(CF_KERNEL):
            a, ph = divmod(HL - CF_KERNEL // 2 + k, SUB)
            acc = acc + wrep[SC_KERNEL + k] * rows3(u_cf[ph, r + SUB * a:r + SUB * a + RB, :])
        acc = acc.reshape(RB, W)
        mu = jnp.mean(acc, axis=-1, keepdims=True)
        d = acc - mu
        var = jnp.mean(d * d, axis=-1, keepdims=True)
        y = (rows3(d * lax.rsqrt(var + EPS)) * wrep[i_lnw] + wrep[i_lnb]).reshape(RB, W)
        f_ref[0, r:r + RB, :] = (_silu(y) * _silu(z_cf[rg:rg + RB])).astype(BF16)


def _conv_branches(xs, mod, nw, w_conv, l, scw, cfw, cfb, lnw, lnb, T):
    B, S, _ = xs.shape
    nT = S // T
    HL = CONV_HALO
    bpt = T // HL
    nblk = S // HL
    wcols = CONV_COLS[1] - CONV_COLS[0]
    vec = lambda: pl.BlockSpec((1, BRANCH_W), lambda b, t: (0, 0))
    return pl.pallas_call(
        functools.partial(_conv_kernel, T, S),
        grid=(B, nT),
        in_specs=[
            pl.BlockSpec((1, T, D_MODEL), lambda b, t: (b, t, 0)),
            pl.BlockSpec((1, HL, D_MODEL), lambda b, t: (b, jnp.maximum(t * bpt - 1, 0), 0)),
            pl.BlockSpec((1, HL, D_MODEL), lambda b, t: (b, jnp.minimum((t + 1) * bpt, nblk - 1), 0)),
            pl.BlockSpec((1, 3, D_MODEL), lambda b, t: (b, 0, 0)),
            pl.BlockSpec((1, D_MODEL), lambda b, t: (0, 0)),
            pl.BlockSpec((1, D_MODEL, wcols), lambda b, t: (l, 0, 0)),
            pl.BlockSpec((SC_KERNEL, BRANCH_W), lambda b, t: (0, 0)),
            pl.BlockSpec((CF_KERNEL, BRANCH_W), lambda b, t: (0, 0)),
            vec(), vec(), vec(),
        ],
        out_specs=[pl.BlockSpec((1, T, BRANCH_W), lambda b, t: (b, t, 0)),
                   pl.BlockSpec((1, T, BRANCH_W), lambda b, t: (b, t, 0))],
        out_shape=[jax.ShapeDtypeStruct((B, S, BRANCH_W), BF16),
                   jax.ShapeDtypeStruct((B, S, BRANCH_W), BF16)],
        scratch_shapes=[
            pltpu.VMEM((T + 2 * HL, D_MODEL), BF16),
            pltpu.VMEM((len(_sc_phases()), T + 2 * HL, BRANCH_W), F32),
            pltpu.VMEM((SUBLANES, T + 2 * HL, BRANCH_W), F32),
            pltpu.VMEM((SC_KERNEL + CF_KERNEL + 3, SUBLANES, BRANCH_W), F32),
        ],
        compiler_params=_params(),
        name="conv_branches",
    )(xs, xs, xs, mod, nw, w_conv, scw, cfw, cfb, lnw, lnb)


def _merge_kernel(final, x_ref, mod_ref, nw_ref, wg_ref, r_ref, s_ref, f_ref, a_ref, wb_ref, wo_ref,
                  fnw_ref, o_ref):
    T = x_ref.shape[1]
    sub = min(T, MERGE_SUB_ROWS)
    for r0 in range(0, T, sub):
        rows = slice(r0, r0 + sub)
        x = x_ref[0, rows, :]
        h = _normed(x, mod_ref, nw_ref)
        merged = None
        for i, br in enumerate((r_ref, s_ref, f_ref, a_ref)):
            g = _dot(h, wg_ref[0, :, i * D_MODEL:(i + 1) * D_MODEL])
            y = _sigmoid(g) * _dot(br[0, rows, :], wb_ref[0, i])
            merged = y if merged is None else merged + y
        out = _dot(merged.astype(BF16), wo_ref[0])
        xn = x + mod_ref[0, 2:3, :] * out
        if final:
            ms = jnp.mean(xn * xn, axis=-1, keepdims=True)
            xn = xn * lax.rsqrt(ms + EPS) * fnw_ref[...]
        o_ref[0, rows, :] = xn


def _merge(xs, mod, nw, w_gate, l, r, s, f, a, w_branch, w_out, fnw, T, final):
    B, S, _ = xs.shape
    nT = S // T
    br = lambda: pl.BlockSpec((1, T, BRANCH_W), lambda b, t: (b, t, 0))
    return pl.pallas_call(
        functools.partial(_merge_kernel, final),
        grid=(B, nT),
        in_specs=[
            pl.BlockSpec((1, T, D_MODEL), lambda b, t: (b, t, 0)),
            pl.BlockSpec((1, 3, D_MODEL), lambda b, t: (b, 0, 0)),
            pl.BlockSpec((1, D_MODEL), lambda b, t: (0, 0)),
            pl.BlockSpec((1, D_MODEL, 4 * D_MODEL), lambda b, t: (l, 0, 0), pipeline_mode=pl.Buffered(1)),
            br(), br(), br(), br(),
            pl.BlockSpec((1, 4, BRANCH_W, D_MODEL), lambda b, t: (l, 0, 0, 0), pipeline_mode=pl.Buffered(1)),
            pl.BlockSpec((1, D_MODEL, D_MODEL), lambda b, t: (l, 0, 0), pipeline_mode=pl.Buffered(1)),
            pl.BlockSpec((1, D_MODEL), lambda b, t: (0, 0)),
        ],
        out_specs=pl.BlockSpec((1, T, D_MODEL), lambda b, t: (b, t, 0)),
        out_shape=jax.ShapeDtypeStruct((B, S, D_MODEL), F32),
        compiler_params=_params(),
        name="merge",
    )(xs, mod, nw, w_gate, r, s, f, a, w_branch, w_out, fnw)


def _rope_table(S):
    rows = S // GRID_W
    row = jnp.repeat(jnp.arange(rows), GRID_W).astype(F32)
    col = jnp.tile(jnp.arange(GRID_W), rows).astype(F32)
    inv = ROPE_BASE ** (-jnp.arange(ROPE_AXIS_FREQS, dtype=F32) / ROPE_AXIS_FREQS)
    ang = jnp.concatenate([row[:, None] * inv[None], col[:, None] * inv[None]], axis=-1)
    cos, sin = jnp.cos(ang), jnp.sin(ang)
    zero = jnp.zeros_like(sin)
    reps = LANES // ATT_DH
    cos_t = jnp.tile(jnp.concatenate([cos, cos], axis=-1), (1, reps))
    sin_lo = jnp.tile(jnp.concatenate([-sin, zero], axis=-1), (1, reps))
    sin_hi = jnp.tile(jnp.concatenate([zero, sin], axis=-1), (1, reps))
    return jnp.stack([cos_t, sin_lo, sin_hi])


def kernel(x, c, ctx, c_ctx, w_mod, b_mod, norm_w, w_in, ret_decay, ret_gn_w, sc_conv_w, cf_conv_w,
           cf_conv_b, cf_ln_w, cf_ln_b, att_sink, w_branch, w_out, final_norm_w):
    B, S, D = x.shape
    L = ctx.shape[1]
    T = TILE_X

    w_ret, w_conv, w_att, w_gate = _split_projection(w_in)
    w_br = w_branch.astype(BF16)
    w_o = w_out.astype(BF16)
    rope_tab = _rope_table(S)

    mod_rows = 16
    cc = jnp.concatenate([c, c_ctx[None], jnp.zeros((mod_rows - B - 1, D), F32)], axis=0)
    mod_all = _modulation(cc, w_mod, b_mod)
    mod_x = mod_all[:, :B].reshape(DEPTH, B, 3, D)
    mod_c = jnp.broadcast_to(mod_all[:, B:B + 1].reshape(DEPTH, 1, 3, D), (DEPTH, B, 3, D))

    zero_state = jnp.zeros((B, RET_HEADS, RET_DH, RET_DH), F32)
    xc = ctx
    for l in range(DEPTH):
        last = l == DEPTH - 1
        nw = norm_w[l][None]
        dec = jnp.broadcast_to(ret_decay[l].reshape(2 * RET_HEADS, 1), (2 * RET_HEADS, LANES))
        gn = ret_gn_w[l][None]
        vecs = (sc_conv_w[l], cf_conv_w[l], cf_conv_b[l][None], cf_ln_w[l][None], cf_ln_b[l][None])
        fnw = final_norm_w[None]

        r_c, s_f, s_b = _retention(xc, mod_c[l], nw, w_ret, l, dec, gn, zero_state, zero_state, L)
        a_c, k_c, v_c = _attention_ctx(xc, mod_c[l], nw, w_att, l, att_sink[l])

        r_x, _, _ = _retention(x, mod_x[l], nw, w_ret, l, dec, gn, s_f, s_b, T)
        a_x = _attention_local(x, mod_x[l], nw, w_att, l, rope_tab, k_c, v_c, att_sink[l], TILE_ATT)
        s_x, f_x = _conv_branches(x, mod_x[l], nw, w_conv, l, *vecs, T)
        x = _merge(x, mod_x[l], nw, w_gate, l, r_x, s_x, f_x, a_x, w_br, w_o, fnw, TILE_MERGE, last)

        if not last:
            s_c, f_c = _conv_branches(xc, mod_c[l], nw, w_conv, l, *vecs, L)
            xc = _merge(xc, mod_c[l], nw, w_gate, l, r_c, s_c, f_c, a_c, w_br, w_o, fnw, L, False)
    return x
```

```python
import functools

import jax
import jax.numpy as jnp
from jax import lax
from jax.experimental import pallas as pl
from jax.experimental.pallas import tpu as pltpu

D_MODEL = 1024
DEPTH = 4
GRID_W = 64
BRANCH_W = D_MODEL // 2
RET_HEADS = 4
RET_DH = BRANCH_W // RET_HEADS
RET_CHUNK = 128
SC_KERNEL = 3
CF_KERNEL = 31
ATT_HEADS = 8
ATT_KV_HEADS = 2
ATT_GROUP = ATT_HEADS // ATT_KV_HEADS
ATT_DH = BRANCH_W // ATT_HEADS
ATT_KV_W = ATT_KV_HEADS * ATT_DH
ATT_BLOCK = 128
WINDOW = 128
ROPE_BASE = 10000.0
ROPE_AXIS_FREQS = ATT_DH // 4
EPS = 1e-6
NEG_INF = -1e30
LOG2E = 1.4426950408889634

RET_COLS = (0, 4 * BRANCH_W)
CONV_COLS = (4 * BRANCH_W, 11 * BRANCH_W)
ATT_COLS = (11 * BRANCH_W, 13 * BRANCH_W + 2 * ATT_KV_W)
GATE_COLS = (13 * BRANCH_W + 2 * ATT_KV_W, 13 * BRANCH_W + 2 * ATT_KV_W + 4 * D_MODEL)

LANES = 128
SUBLANES = 8
CONV_HALO = 16
CONV_ROW_BLOCK = 64
TILE_X = 512
TILE_ATT = 1024
TILE_MERGE = 1024
MERGE_SUB_ROWS = 512
VMEM_LIMIT = 56 * 1024 * 1024

F32 = jnp.float32
BF16 = jnp.bfloat16


def _dot(a, b):
    return jnp.dot(a, b, preferred_element_type=F32)


def _dot_nt(a, b):
    return lax.dot_general(a, b, (((1,), (1,)), ((), ())), preferred_element_type=F32)


def _sigmoid(v):
    return jax.nn.sigmoid(v)


def _silu(v):
    return v * _sigmoid(v)


def _normed(x, mod_ref, nw_ref):
    a = nw_ref[...] * (1.0 + mod_ref[0, 1:2, :])
    shift = mod_ref[0, 0:1, :]
    ms = jnp.mean(x * x, axis=-1, keepdims=True)
    return (x * lax.rsqrt(ms + EPS) * a + shift).astype(BF16)


def _params():
    return pltpu.CompilerParams(vmem_limit_bytes=VMEM_LIMIT)


def _mod_kernel(c_ref, w_ref, b_ref, o_ref):
    cv = c_ref[...]
    o_ref[0] = _dot(_silu(cv).astype(BF16), w_ref[0].astype(BF16)) + b_ref[0]


def _modulation(cc, w_mod, b_mod):
    rows = cc.shape[0]
    tn = 1024
    return pl.pallas_call(
        _mod_kernel,
        grid=(DEPTH, 3 * D_MODEL // tn),
        in_specs=[pl.BlockSpec((rows, D_MODEL), lambda l, n: (0, 0)),
                  pl.BlockSpec((1, D_MODEL, tn), lambda l, n: (l, 0, n)),
                  pl.BlockSpec((1, 1, tn), lambda l, n: (l, 0, n))],
        out_specs=pl.BlockSpec((1, rows, tn), lambda l, n: (l, 0, n)),
        out_shape=jax.ShapeDtypeStruct((DEPTH, rows, 3 * D_MODEL), F32),
        compiler_params=_params(),
        name="modulation",
    )(cc, w_mod, b_mod.reshape(DEPTH, 1, 3 * D_MODEL))


_W_GROUPS = (RET_COLS, CONV_COLS, ATT_COLS, GATE_COLS)


def _split_kernel(w_ref, *outs):
    for (c0, c1), o in zip(_W_GROUPS, outs):
        o[0] = w_ref[0, :, c0:c1].astype(BF16)


def _split_projection(w_in):
    rows = 256
    return pl.pallas_call(
        _split_kernel,
        grid=(DEPTH, D_MODEL // rows),
        in_specs=[pl.BlockSpec((1, rows, w_in.shape[-1]), lambda l, r: (l, r, 0))],
        out_specs=[pl.BlockSpec((1, rows, c1 - c0), lambda l, r: (l, r, 0)) for c0, c1 in _W_GROUPS],
        out_shape=[jax.ShapeDtypeStruct((DEPTH, D_MODEL, c1 - c0), BF16) for c0, c1 in _W_GROUPS],
        compiler_params=_params(),
        name="split_projection",
    )(w_in)


_T_DEC, _T_QF, _T_KF, _T_QB, _T_KB, _T_CF, _T_CB = range(7)


def _ret_kernel(T, S, x_ref, mod_ref, nw_ref, w_ref, dec_ref, gn_ref, s0f_ref, s0b_ref,
                r_ref, sf_ref, sb_ref, qkv, zbuf, obuf, st_f, st_b, tab):
    t = pl.program_id(1)
    nT = S // T
    C = RET_CHUNK
    H = RET_HEADS
    k_scale = RET_DH ** -0.5

    @pl.when(t == 0)
    def _init():
        raw = dec_ref[...]
        lg = jnp.minimum(raw, 0.0) - jnp.log(1.0 + jnp.exp(-jnp.abs(raw)))
        ii = lax.broadcasted_iota(jnp.int32, (C, C), 0)
        jj = lax.broadcasted_iota(jnp.int32, (C, C), 1)
        diff = (ii - jj).astype(F32)
        pos = ii.astype(F32)
        for h in range(H):
            lf = lg[h:h + 1, :]
            lb = lg[H + h:H + h + 1, :]
            dec_f = jnp.where(diff >= 0, jnp.exp(jnp.where(diff >= 0, diff, 0.0) * lf), 0.0)
            dec_b = jnp.where(diff < 0, jnp.exp(jnp.where(diff < 0, -diff, 0.0) * lb), 0.0)
            tab[_T_DEC * H + h] = (dec_f + dec_b) * k_scale
            tab[_T_QF * H + h] = jnp.exp((pos + 1.0) * lf)
            tab[_T_KF * H + h] = jnp.exp((C - 1.0 - pos) * lf) * k_scale
            tab[_T_QB * H + h] = jnp.exp((C - pos) * lb)
            tab[_T_KB * H + h] = jnp.exp(pos * lb) * k_scale
            tab[_T_CF * H + h] = jnp.exp(float(C) * lf) + jnp.zeros((C, LANES), F32)
            tab[_T_CB * H + h] = jnp.exp(float(C) * lb) + jnp.zeros((C, LANES), F32)
        st_f[...] = s0f_ref[0]
        st_b[...] = s0b_ref[0]

    def pair(rows, part, hp):
        c0 = part * BRANCH_W + hp * RET_DH
        return qkv[rows, c0:c0 + 2 * RET_DH]

    def tab2(slot, hp):
        return jnp.concatenate([tab[slot * H + hp], tab[slot * H + hp + 1]], axis=1)

    def blockdiag(ab):
        n, m = ab.shape[0], ab.shape[1] // 2
        z = jnp.zeros((n, m), ab.dtype)
        return jnp.concatenate([jnp.concatenate([ab[:, :m], z], axis=1),
                                jnp.concatenate([z, ab[:, m:]], axis=1)], axis=0)

    def transpose2(ab):
        m = ab.shape[1] // 2
        return jnp.concatenate([jnp.transpose(ab[:, :m]), jnp.transpose(ab[:, m:])], axis=1)

    @pl.when(t < nT)
    def _forward():
        h_tile = _normed(x_ref[0], mod_ref, nw_ref)
        proj = _dot(h_tile, w_ref[0])
        row0 = pl.multiple_of(t * T, T)
        qkv[pl.ds(row0, T), :] = proj[:, :3 * BRANCH_W].astype(BF16)
        zbuf[pl.ds(row0, T), :] = proj[:, 3 * BRANCH_W:]
        for c in range(T // C):
            rows = pl.ds(pl.multiple_of(t * T + c * C, C), C)
            for hp in range(0, H, 2):
                q2, k2, v2 = pair(rows, 0, hp), pair(rows, 1, hp), pair(rows, 2, hp)
                sc2 = (_dot_nt(q2, blockdiag(k2)) * tab2(_T_DEC, hp)).astype(BF16)
                qs2 = (q2.astype(F32) * tab2(_T_QF, hp)).astype(BF16)
                kwt2 = transpose2(k2.astype(F32) * tab2(_T_KF, hp)).astype(BF16)
                res = _dot(jnp.concatenate([sc2, kwt2], axis=0), blockdiag(v2))
                cross = _dot(qs2, blockdiag(jnp.concatenate([st_f[hp], st_f[hp + 1]], axis=1).astype(BF16)))
                obuf[rows, hp * RET_DH:(hp + 2) * RET_DH] = res[:C] + cross
                for i in range(2):
                    st_f[hp + i] = st_f[hp + i] * tab[_T_CF * H + hp + i] + res[C:, i * RET_DH:(i + 1) * RET_DH]

        @pl.when(t == nT - 1)
        def _():
            sf_ref[0] = st_f[...]

    @pl.when(t == nT)
    def _backward():
        def chunk(i, carry):
            rows = pl.ds(pl.multiple_of((S // C - 1 - i) * C, C), C)
            for h in range(H):
                hc = slice(h * RET_DH, (h + 1) * RET_DH)
                q = qkv[rows, hc]
                k = qkv[rows, BRANCH_W + h * RET_DH:BRANCH_W + (h + 1) * RET_DH]
                v = qkv[rows, 2 * BRANCH_W + h * RET_DH:2 * BRANCH_W + (h + 1) * RET_DH]
                qs = (q.astype(F32) * tab[_T_QB * H + h]).astype(BF16)
                kwt = jnp.transpose(k.astype(F32) * tab[_T_KB * H + h]).astype(BF16)
                zero = jnp.zeros((C, RET_DH), BF16)
                lhs = jnp.concatenate([jnp.concatenate([qs, zero], axis=1),
                                       jnp.concatenate([zero, kwt], axis=1)], axis=0)
                res = _dot(lhs, jnp.concatenate([st_b[h].astype(BF16), v], axis=0))
                o = obuf[rows, hc] + res[:C]
                mu = jnp.mean(o, axis=-1, keepdims=True)
                d = o - mu
                var = jnp.mean(d * d, axis=-1, keepdims=True)
                y = d * lax.rsqrt(var + EPS) * gn_ref[:, hc]
                r_ref[0, rows, hc] = (y * _silu(zbuf[rows, hc])).astype(BF16)
                st_b[h] = st_b[h] * tab[_T_CB * H + h] + res[C:]
            return carry

        lax.fori_loop(0, S // C, chunk, 0, unroll=min(4, S // C))
        sb_ref[0] = st_b[...]


def _retention(xs, mod, nw, w_ret, l, dec, gn_w, s0f, s0b, T):
    B, S, _ = xs.shape
    nT = S // T
    st_shape = (RET_HEADS, RET_DH, RET_DH)
    wcols = RET_COLS[1] - RET_COLS[0]
    return pl.pallas_call(
        functools.partial(_ret_kernel, T, S),
        grid=(B, nT + 1),
        in_specs=[
            pl.BlockSpec((1, T, D_MODEL), lambda b, t: (b, jnp.minimum(t, nT - 1), 0)),
            pl.BlockSpec((1, 3, D_MODEL), lambda b, t: (b, 0, 0)),
            pl.BlockSpec((1, D_MODEL), lambda b, t: (0, 0)),
            pl.BlockSpec((1, D_MODEL, wcols), lambda b, t: (l, 0, 0), pipeline_mode=pl.Buffered(1)),
            pl.BlockSpec((2 * RET_HEADS, LANES), lambda b, t: (0, 0)),
            pl.BlockSpec((1, BRANCH_W), lambda b, t: (0, 0)),
            pl.BlockSpec((1,) + st_shape, lambda b, t: (b, 0, 0, 0)),
            pl.BlockSpec((1,) + st_shape, lambda b, t: (b, 0, 0, 0)),
        ],
        out_specs=[
            pl.BlockSpec((1, S, BRANCH_W), lambda b, t: (b, 0, 0)),
            pl.BlockSpec((1,) + st_shape, lambda b, t: (b, 0, 0, 0)),
            pl.BlockSpec((1,) + st_shape, lambda b, t: (b, 0, 0, 0)),
        ],
        out_shape=[
            jax.ShapeDtypeStruct((B, S, BRANCH_W), BF16),
            jax.ShapeDtypeStruct((B,) + st_shape, F32),
            jax.ShapeDtypeStruct((B,) + st_shape, F32),
        ],
        scratch_shapes=[
            pltpu.VMEM((S, 3 * BRANCH_W), BF16),
            pltpu.VMEM((S, BRANCH_W), F32),
            pltpu.VMEM((S, BRANCH_W), F32),
            pltpu.VMEM(st_shape, F32),
            pltpu.VMEM(st_shape, F32),
            pltpu.VMEM((7 * RET_HEADS, RET_CHUNK, LANES), F32),
        ],
        compiler_params=_params(),
        name="retention",
    )(xs, mod, nw, w_ret, dec, gn_w, s0f, s0b)


def _rope(val, rope_ref, r0, nrows):
    cos = rope_ref[0, pl.ds(r0, nrows), :]
    sin_lo = rope_ref[1, pl.ds(r0, nrows), :]
    sin_hi = rope_ref[2, pl.ds(r0, nrows), :]
    outs = []
    for g in range(val.shape[1] // LANES):
        vg = val[:, g * LANES:(g + 1) * LANES]
        outs.append(vg * cos + pltpu.roll(vg, LANES - ATT_DH // 2, 1) * sin_lo
                    + pltpu.roll(vg, ATT_DH // 2, 1) * sin_hi)
    return outs[0] if len(outs) == 1 else jnp.concatenate(outs, axis=1)


def _softmax_av(qh, parts, sink):
    scores = []
    m = None
    for k, _, mask in parts:
        s = _dot_nt(qh, k)
        if mask is not None:
            s = jnp.where(mask, s, NEG_INF)
        scores.append(s)
        sm = jnp.max(s, axis=-1, keepdims=True)
        m = sm if m is None else jnp.maximum(m, sm)
    m = jnp.maximum(m, sink)
    den = jnp.exp(sink - m)
    o = None
    for s, (_, v, _) in zip(scores, parts):
        pr = jnp.exp(s - m)
        den = den + jnp.sum(pr, axis=-1, keepdims=True)
        ov = _dot(pr.astype(BF16), v)
        o = ov if o is None else o + ov
    return o / den


def _v_ext_t(v, hk):
    lane = lax.broadcasted_iota(jnp.int32, v.shape, 1)
    vh = v if hk == 0 else pltpu.roll(v, ATT_DH, 1)
    return jnp.transpose(jnp.where(lane < ATT_DH, vh, 1.0)).astype(BF16)


def _att_local_kernel(T, S, x_ref, xp_ref, xn_ref, mod_ref, nw_ref, w_ref, rope_ref, kc_ref, vct_ref,
                      sink_ref, a_ref, q_s, k_s, vt_s, z_s, ot_s):
    t = pl.program_id(1)
    nb = T // ATT_BLOCK
    blk = ATT_BLOCK
    G = ATT_GROUP
    q_cols = slice(0, BRANCH_W)
    kv_cols = slice(BRANCH_W, BRANCH_W + 2 * ATT_KV_W)
    z_cols = slice(BRANCH_W + 2 * ATT_KV_W, 2 * BRANCH_W + 2 * ATT_KV_W)

    def put_kv(kv, r0, dst):
        k = _rope(kv[:, :ATT_KV_W], rope_ref, r0, kv.shape[0])
        for hk in range(ATT_KV_HEADS):
            k_s[hk, dst, :] = k[:, hk * ATT_DH:(hk + 1) * ATT_DH].astype(BF16)
            vt_s[hk, :, dst] = _v_ext_t(kv[:, ATT_KV_W:], hk)

    hm = _normed(x_ref[0], mod_ref, nw_ref)
    row0 = pl.multiple_of(t * T, T)
    q = _rope(_dot(hm, w_ref[0, :, q_cols]), rope_ref, row0, T) * (ATT_DH ** -0.5 * LOG2E)
    for j in range(nb):
        for hd in range(ATT_HEADS):
            hk, g = divmod(hd, G)
            q_s[j * ATT_KV_HEADS + hk, g * blk:(g + 1) * blk, :] = (
                q[j * blk:(j + 1) * blk, hd * ATT_DH:(hd + 1) * ATT_DH].astype(BF16))
    put_kv(_dot(hm, w_ref[0, :, kv_cols]), row0, slice(blk, blk + T))
    z_s[...] = _silu(_dot(hm, w_ref[0, :, z_cols]))

    rp = pl.multiple_of(jnp.maximum(t * T - blk, 0), blk)
    put_kv(_dot(_normed(xp_ref[0], mod_ref, nw_ref), w_ref[0, :, kv_cols]), rp, slice(0, blk))
    rn = pl.multiple_of(jnp.minimum((t + 1) * T, S - blk), blk)
    put_kv(_dot(_normed(xn_ref[0], mod_ref, nw_ref), w_ref[0, :, kv_cols]), rn, slice(blk + T, 2 * blk + T))

    NQ = G * blk
    kk = lax.broadcasted_iota(jnp.int32, (blk, NQ), 0)
    qq = lax.broadcasted_iota(jnp.int32, (blk, NQ), 1) & (blk - 1)
    grp = lax.broadcasted_iota(jnp.int32, (1, NQ), 1) // blk
    sinks = []
    for hk in range(ATT_KV_HEADS):
        sink = jnp.full((1, NQ), sink_ref[hk * G], F32)
        for g in range(1, G):
            sink = jnp.where(grp == g, sink_ref[hk * G + g], sink)
        sinks.append(sink * LOG2E)
    for j in range(nb):
        gb = t * nb + j
        wrows = slice(j * blk, (j + 3) * blk)
        qrows = slice(j * blk, (j + 1) * blk)
        for hk in range(ATT_KV_HEADS):
            q4 = q_s[j * ATT_KV_HEADS + hk]
            s_loc = _dot_nt(k_s[hk, wrows, :], q4)
            s_prev = jnp.where((kk >= qq) & (gb > 0), s_loc[:blk], NEG_INF)
            s_cur = s_loc[blk:2 * blk]
            s_next = jnp.where((kk <= qq) & (gb < S // blk - 1), s_loc[2 * blk:], NEG_INF)
            s_ctx = _dot_nt(kc_ref[0, hk], q4)
            m = jnp.maximum(jnp.maximum(s_prev, s_cur), s_next)
            for cb in range(s_ctx.shape[0] // blk):
                m = jnp.maximum(m, s_ctx[cb * blk:(cb + 1) * blk])
            m = jnp.maximum(jnp.max(m, axis=0, keepdims=True), sinks[hk])
            p_loc = jnp.concatenate([jnp.exp2(s_prev - m), jnp.exp2(s_cur - m), jnp.exp2(s_next - m)], axis=0)
            o = (_dot(vt_s[hk, :, wrows], p_loc.astype(BF16))
                 + _dot(vct_ref[0, hk], jnp.exp2(s_ctx - m).astype(BF16)))
            res = o[:ATT_DH] / (o[ATT_DH:] + jnp.exp2(sinks[hk] - m))
            for g in range(G):
                hd = hk * G + g
                ot_s[j, hd * ATT_DH:(hd + 1) * ATT_DH, :] = res[:, g * blk:(g + 1) * blk]
        a_ref[0, qrows, :] = (jnp.transpose(ot_s[j]) * z_s[qrows, :]).astype(BF16)


def _att_ctx_kernel(x_ref, mod_ref, nw_ref, w_ref, sink_ref, a_ref, kc_ref, vct_ref, q_s, z_s):
    q_cols = slice(0, BRANCH_W)
    kv_cols = slice(BRANCH_W, BRANCH_W + 2 * ATT_KV_W)
    z_cols = slice(BRANCH_W + 2 * ATT_KV_W, 2 * BRANCH_W + 2 * ATT_KV_W)
    hm = _normed(x_ref[0], mod_ref, nw_ref)
    q_s[...] = (_dot(hm, w_ref[0, :, q_cols]) * (ATT_DH ** -0.5)).astype(BF16)
    kv = _dot(hm, w_ref[0, :, kv_cols])
    for hk in range(ATT_KV_HEADS):
        kc_ref[0, hk] = kv[:, hk * ATT_DH:(hk + 1) * ATT_DH].astype(BF16)
        vct_ref[0, hk] = _v_ext_t(kv[:, ATT_KV_W:], hk)
    z_s[...] = _silu(_dot(hm, w_ref[0, :, z_cols]))
    for hk in range(ATT_KV_HEADS):
        v_hk = kv[:, ATT_KV_W + hk * ATT_DH:ATT_KV_W + (hk + 1) * ATT_DH].astype(BF16)
        parts = [(kc_ref[0, hk], v_hk, None)]
        for g in range(ATT_GROUP):
            hd = hk * ATT_GROUP + g
            hc = slice(hd * ATT_DH, (hd + 1) * ATT_DH)
            o = _softmax_av(q_s[:, hc], parts, sink_ref[hd])
            a_ref[0, :, hc] = (o * z_s[:, hc]).astype(BF16)


def _attention_local(xs, mod, nw, w_att, l, rope_tab, kc, vc, sink, T):
    B, S, _ = xs.shape
    nT = S // T
    bpt = T // ATT_BLOCK
    nblk = S // ATT_BLOCK
    L = kc.shape[2]
    wcols = ATT_COLS[1] - ATT_COLS[0]
    return pl.pallas_call(
        functools.partial(_att_local_kernel, T, S),
        grid=(B, nT),
        in_specs=[
            pl.BlockSpec((1, T, D_MODEL), lambda b, t: (b, t, 0)),
            pl.BlockSpec((1, ATT_BLOCK, D_MODEL), lambda b, t: (b, jnp.maximum(t * bpt - 1, 0), 0)),
            pl.BlockSpec((1, ATT_BLOCK, D_MODEL), lambda b, t: (b, jnp.minimum((t + 1) * bpt, nblk - 1), 0)),
            pl.BlockSpec((1, 3, D_MODEL), lambda b, t: (b, 0, 0)),
            pl.BlockSpec((1, D_MODEL), lambda b, t: (0, 0)),
            pl.BlockSpec((1, D_MODEL, wcols), lambda b, t: (l, 0, 0)),
            pl.BlockSpec((3, S, LANES), lambda b, t: (0, 0, 0)),
            pl.BlockSpec((1, ATT_KV_HEADS, L, ATT_DH), lambda b, t: (b, 0, 0, 0)),
            pl.BlockSpec((1, ATT_KV_HEADS, 2 * ATT_DH, L), lambda b, t: (b, 0, 0, 0)),
            pl.BlockSpec(memory_space=pltpu.SMEM),
        ],
        out_specs=pl.BlockSpec((1, T, BRANCH_W), lambda b, t: (b, t, 0)),
        out_shape=jax.ShapeDtypeStruct((B, S, BRANCH_W), BF16),
        scratch_shapes=[
            pltpu.VMEM((bpt * ATT_KV_HEADS, ATT_GROUP * ATT_BLOCK, ATT_DH), BF16),
            pltpu.VMEM((ATT_KV_HEADS, T + 2 * ATT_BLOCK, ATT_DH), BF16),
            pltpu.VMEM((ATT_KV_HEADS, 2 * ATT_DH, T + 2 * ATT_BLOCK), BF16),
            pltpu.VMEM((T, BRANCH_W), F32),
            pltpu.VMEM((bpt, BRANCH_W, ATT_BLOCK), F32),
        ],
        compiler_params=_params(),
        name="attention_local",
    )(xs, xs, xs, mod, nw, w_att, rope_tab, kc, vc, sink)


def _attention_ctx(xc, mod, nw, w_att, l, sink):
    B, L, _ = xc.shape
    wcols = ATT_COLS[1] - ATT_COLS[0]
    return pl.pallas_call(
        _att_ctx_kernel,
        grid=(B,),
        in_specs=[
            pl.BlockSpec((1, L, D_MODEL), lambda b: (b, 0, 0)),
            pl.BlockSpec((1, 3, D_MODEL), lambda b: (b, 0, 0)),
            pl.BlockSpec((1, D_MODEL), lambda b: (0, 0)),
            pl.BlockSpec((1, D_MODEL, wcols), lambda b: (l, 0, 0)),
            pl.BlockSpec(memory_space=pltpu.SMEM),
        ],
        out_specs=[
            pl.BlockSpec((1, L, BRANCH_W), lambda b: (b, 0, 0)),
            pl.BlockSpec((1, ATT_KV_HEADS, L, ATT_DH), lambda b: (b, 0, 0, 0)),
            pl.BlockSpec((1, ATT_KV_HEADS, 2 * ATT_DH, L), lambda b: (b, 0, 0, 0)),
        ],
        out_shape=[
            jax.ShapeDtypeStruct((B, L, BRANCH_W), BF16),
            jax.ShapeDtypeStruct((B, ATT_KV_HEADS, L, ATT_DH), BF16),
            jax.ShapeDtypeStruct((B, ATT_KV_HEADS, 2 * ATT_DH, L), BF16),
        ],
        scratch_shapes=[
            pltpu.VMEM((L, BRANCH_W), BF16),
            pltpu.VMEM((L, BRANCH_W), F32),
        ],
        compiler_params=_params(),
        name="attention_ctx",
    )(xc, mod, nw, w_att, sink)


def _sc_phases():
    return tuple(sorted({(CONV_HALO - SC_KERNEL // 2 + k) % SUBLANES for k in range(SC_KERNEL)}))


def _conv_kernel(T, S, x_ref, xp_ref, xn_ref, mod_ref, nw_ref, w_ref, scw_ref, cfw_ref, cfb_ref,
                 lnw_ref, lnb_ref, s_ref, f_ref, hbuf, u_sc, u_cf, wrep):
    t = pl.program_id(1)
    nT = S // T
    HL = CONV_HALO
    W = BRANCH_W
    E = T + 2 * HL
    SUB = SUBLANES
    sc_phases = _sc_phases()
    hbuf[0:HL, :] = _normed(xp_ref[0], mod_ref, nw_ref)
    hbuf[HL:HL + T, :] = _normed(x_ref[0], mod_ref, nw_ref)
    hbuf[HL + T:2 * HL + T, :] = _normed(xn_ref[0], mod_ref, nw_ref)
    hext = hbuf[...]
    hm = hbuf[HL:HL + T, :]

    row = lax.broadcasted_iota(jnp.int32, (E, 1), 0)
    in_seq = ((row >= HL) | (t > 0)) & ((row < HL + T) | (t < nT - 1))

    def store_phases(u, dst, phases):
        for i, ph in enumerate(phases):
            dst[i, 0:E - SUB, :] = u[ph:ph + E - SUB, :]

    cx = _dot(hext, w_ref[0, :, W:3 * W])
    store_phases(jnp.where(in_seq, cx[:, :W] * cx[:, W:], 0.0), u_sc, sc_phases)
    glu = _dot(hext, w_ref[0, :, 4 * W:6 * W])
    store_phases(jnp.where(in_seq, glu[:, :W] * _sigmoid(glu[:, W:]), 0.0), u_cf, range(SUB))

    vec_rows = ([scw_ref[k:k + 1, :] for k in range(SC_KERNEL)] + [cfw_ref[k:k + 1, :] for k in range(CF_KERNEL)]
                + [cfb_ref[...], lnw_ref[...], lnb_ref[...]])
    for i, v in enumerate(vec_rows):
        wrep[i] = jnp.broadcast_to(v, (SUB, W))
    i_cfb, i_lnw, i_lnb = (SC_KERNEL + CF_KERNEL + i for i in range(3))

    RB = CONV_ROW_BLOCK

    def rows3(v):
        return v.reshape(RB // SUB, SUB, W)

    GR = 256
    for rb in range(T // RB):
        r = rb * RB
        if r % GR == 0:
            hg = hbuf[HL + r:HL + r + GR, :]
            b_gate = _dot(hg, w_ref[0, :, 0:W])
            z_sc = _dot(hg, w_ref[0, :, 3 * W:4 * W])
            z_cf = _dot(hg, w_ref[0, :, 6 * W:7 * W])
        rg = r % GR
        acc = None
        for k in range(SC_KERNEL):
            a, ph = divmod(HL - SC_KERNEL // 2 + k, SUB)
            term = wrep[k] * rows3(u_sc[sc_phases.index(ph), r + SUB * a:r + SUB * a + RB, :])
            acc = term if acc is None else acc + term
        acc = acc.reshape(RB, W)
        s_ref[0, r:r + RB, :] = (b_gate[rg:rg + RB] * acc * _silu(z_sc[rg:rg + RB])).astype(BF16)

        acc = wrep[i_cfb]
        for k in range(CF_KERNEL):
            a, ph = divmod(HL - CF_KERNEL // 2 + k, SUB)
            acc = acc + wrep[SC_KERNEL + k] * rows3(u_cf[ph, r + SUB * a:r + SUB * a + RB, :])
        acc = acc.reshape(RB, W)
        mu = jnp.mean(acc, axis=-1, keepdims=True)
        d = acc - mu
        var = jnp.mean(d * d, axis=-1, keepdims=True)
        y = (rows3(d * lax.rsqrt(var + EPS)) * wrep[i_lnw] + wrep[i_lnb]).reshape(RB, W)
        f_ref[0, r:r + RB, :] = (_silu(y) * _silu(z_cf[rg:rg + RB])).astype(BF16)


def _conv_branches(xs, mod, nw, w_conv, l, scw, cfw, cfb, lnw, lnb, T):
    B, S, _ = xs.shape
    nT = S // T
    HL = CONV_HALO
    bpt = T // HL
    nblk = S // HL
    wcols = CONV_COLS[1] - CONV_COLS[0]
    vec = lambda: pl.BlockSpec((1, BRANCH_W), lambda b, t: (0, 0))
    return pl.pallas_call(
        functools.partial(_conv_kernel, T, S),
        grid=(B, nT),
        in_specs=[
            pl.BlockSpec((1, T, D_MODEL), lambda b, t: (b, t, 0)),
            pl.BlockSpec((1, HL, D_MODEL), lambda b, t: (b, jnp.maximum(t * bpt - 1, 0), 0)),
            pl.BlockSpec((1, HL, D_MODEL), lambda b, t: (b, jnp.minimum((t + 1) * bpt, nblk - 1), 0)),
            pl.BlockSpec((1, 3, D_MODEL), lambda b, t: (b, 0, 0)),
            pl.BlockSpec((1, D_MODEL), lambda b, t: (0, 0)),
            pl.BlockSpec((1, D_MODEL, wcols), lambda b, t: (l, 0, 0)),
            pl.BlockSpec((SC_KERNEL, BRANCH_W), lambda b, t: (0, 0)),
            pl.BlockSpec((CF_KERNEL, BRANCH_W), lambda b, t: (0, 0)),
            vec(), vec(), vec(),
        ],
        out_specs=[pl.BlockSpec((1, T, BRANCH_W), lambda b, t: (b, t, 0)),
                   pl.BlockSpec((1, T, BRANCH_W), lambda b, t: (b, t, 0))],
        out_shape=[jax.ShapeDtypeStruct((B, S, BRANCH_W), BF16),
                   jax.ShapeDtypeStruct((B, S, BRANCH_W), BF16)],
        scratch_shapes=[
            pltpu.VMEM((T + 2 * HL, D_MODEL), BF16),
            pltpu.VMEM((len(_sc_phases()), T + 2 * HL, BRANCH_W), F32),
            pltpu.VMEM((SUBLANES, T + 2 * HL, BRANCH_W), F32),
            pltpu.VMEM((SC_KERNEL + CF_KERNEL + 3, SUBLANES, BRANCH_W), F32),
        ],
        compiler_params=_params(),
        name="conv_branches",
    )(xs, xs, xs, mod, nw, w_conv, scw, cfw, cfb, lnw, lnb)


def _merge_kernel(final, x_ref, mod_ref, nw_ref, wg_ref, r_ref, s_ref, f_ref, a_ref, wb_ref, wo_ref,
                  fnw_ref, o_ref):
    T = x_ref.shape[1]
    sub = min(T, MERGE_SUB_ROWS)
    for r0 in range(0, T, sub):
        rows = slice(r0, r0 + sub)
        x = x_ref[0, rows, :]
        h = _normed(x, mod_ref, nw_ref)
        merged = None
        for i, br in enumerate((r_ref, s_ref, f_ref, a_ref)):
            g = _dot(h, wg_ref[0, :, i * D_MODEL:(i + 1) * D_MODEL])
            y = _sigmoid(g) * _dot(br[0, rows, :], wb_ref[0, i])
            merged = y if merged is None else merged + y
        out = _dot(merged.astype(BF16), wo_ref[0])
        xn = x + mod_ref[0, 2:3, :] * out
        if final:
            ms = jnp.mean(xn * xn, axis=-1, keepdims=True)
            xn = xn * lax.rsqrt(ms + EPS) * fnw_ref[...]
        o_ref[0, rows, :] = xn


def _merge(xs, mod, nw, w_gate, l, r, s, f, a, w_branch, w_out, fnw, T, final):
    B, S, _ = xs.shape
    nT = S // T
    br = lambda: pl.BlockSpec((1, T, BRANCH_W), lambda b, t: (b, t, 0))
    return pl.pallas_call(
        functools.partial(_merge_kernel, final),
        grid=(B, nT),
        in_specs=[
            pl.BlockSpec((1, T, D_MODEL), lambda b, t: (b, t, 0)),
            pl.BlockSpec((1, 3, D_MODEL), lambda b, t: (b, 0, 0)),
            pl.BlockSpec((1, D_MODEL), lambda b, t: (0, 0)),
            pl.BlockSpec((1, D_MODEL, 4 * D_MODEL), lambda b, t: (l, 0, 0), pipeline_mode=pl.Buffered(1)),
            br(), br(), br(), br(),
            pl.BlockSpec((1, 4, BRANCH_W, D_MODEL), lambda b, t: (l, 0, 0, 0), pipeline_mode=pl.Buffered(1)),
            pl.BlockSpec((1, D_MODEL, D_MODEL), lambda b, t: (l, 0, 0), pipeline_mode=pl.Buffered(1)),
            pl.BlockSpec((1, D_MODEL), lambda b, t: (0, 0)),
        ],
        out_specs=pl.BlockSpec((1, T, D_MODEL), lambda b, t: (b, t, 0)),
        out_shape=jax.ShapeDtypeStruct((B, S, D_MODEL), F32),
        compiler_params=_params(),
        name="merge",
    )(xs, mod, nw, w_gate, r, s, f, a, w_branch, w_out, fnw)


def _rope_table(S):
    rows = S // GRID_W
    row = jnp.repeat(jnp.arange(rows), GRID_W).astype(F32)
    col = jnp.tile(jnp.arange(GRID_W), rows).astype(F32)
    inv = ROPE_BASE ** (-jnp.arange(ROPE_AXIS_FREQS, dtype=F32) / ROPE_AXIS_FREQS)
    ang = jnp.concatenate([row[:, None] * inv[None], col[:, None] * inv[None]], axis=-1)
    cos, sin = jnp.cos(ang), jnp.sin(ang)
    zero = jnp.zeros_like(sin)
    reps = LANES // ATT_DH
    cos_t = jnp.tile(jnp.concatenate([cos, cos], axis=-1), (1, reps))
    sin_lo = jnp.tile(jnp.concatenate([-sin, zero], axis=-1), (1, reps))
    sin_hi = jnp.tile(jnp.concatenate([zero, sin], axis=-1), (1, reps))
    return jnp.stack([cos_t, sin_lo, sin_hi])


def kernel(x, c, ctx, c_ctx, w_mod, b_mod, norm_w, w_in, ret_decay, ret_gn_w, sc_conv_w, cf_conv_w,
           cf_conv_b, cf_ln_w, cf_ln_b, att_sink, w_branch, w_out, final_norm_w):
    B, S, D = x.shape
    L = ctx.shape[1]
    T = TILE_X

    w_ret, w_conv, w_att, w_gate = _split_projection(w_in)
    w_br = w_branch.astype(BF16)
    w_o = w_out.astype(BF16)
    rope_tab = _rope_table(S)

    mod_rows = 16
    cc = jnp.concatenate([c, c_ctx[None], jnp.zeros((mod_rows - B - 1, D), F32)], axis=0)
    mod_all = _modulation(cc, w_mod, b_mod)
    mod_x = mod_all[:, :B].reshape(DEPTH, B, 3, D)
    mod_c = jnp.broadcast_to(mod_all[:, B:B + 1].reshape(DEPTH, 1, 3, D), (DEPTH, B, 3, D))

    zero_state = jnp.zeros((B, RET_HEADS, RET_DH, RET_DH), F32)
    xc = ctx
    for l in range(DEPTH):
        last = l == DEPTH - 1
        nw = norm_w[l][None]
        dec = jnp.broadcast_to(ret_decay[l].reshape(2 * RET_HEADS, 1), (2 * RET_HEADS, LANES))
        gn = ret_gn_w[l][None]
        vecs = (sc_conv_w[l], cf_conv_w[l], cf_conv_b[l][None], cf_ln_w[l][None], cf_ln_b[l][None])
        fnw = final_norm_w[None]

        r_c, s_f, s_b = _retention(xc, mod_c[l], nw, w_ret, l, dec, gn, zero_state, zero_state, L)
        a_c, k_c, v_c = _attention_ctx(xc, mod_c[l], nw, w_att, l, att_sink[l])

        r_x, _, _ = _retention(x, mod_x[l], nw, w_ret, l, dec, gn, s_f, s_b, T)
        a_x = _attention_local(x, mod_x[l], nw, w_att, l, rope_tab, k_c, v_c, att_sink[l], TILE_ATT)
        s_x, f_x = _conv_branches(x, mod_x[l], nw, w_conv, l, *vecs, T)
        x = _merge(x, mod_x[l], nw, w_gate, l, r_x, s_x, f_x, a_x, w_br, w_o, fnw, TILE_MERGE, last)

        if not last:
            s_c, f_c = _conv_branches(xc, mod_c[l], nw, w_conv, l, *vecs, L)
            xc = _merge(xc, mod_c[l], nw, w_gate, l, r_c, s_c, f_c, a_c, w_br, w_o, fnw, L, False)
    return x
```

```python
import functools

import jax
import jax.numpy as jnp
from jax import lax
from jax.experimental import pallas as pl
from jax.experimental.pallas import tpu as pltpu

D_MODEL = 1024
DEPTH = 4
GRID_W = 64
BRANCH_W = D_MODEL // 2
RET_HEADS = 4
RET_DH = BRANCH_W // RET_HEADS
RET_CHUNK = 128
SC_KERNEL = 3
CF_KERNEL = 31
ATT_HEADS = 8
ATT_KV_HEADS = 2
ATT_GROUP = ATT_HEADS // ATT_KV_HEADS
ATT_DH = BRANCH_W // ATT_HEADS
ATT_KV_W = ATT_KV_HEADS * ATT_DH
ATT_BLOCK = 128
WINDOW = 128
ROPE_BASE = 10000.0
ROPE_AXIS_FREQS = ATT_DH // 4
EPS = 1e-6
NEG_INF = -1e30
LOG2E = 1.4426950408889634

RET_COLS = (0, 4 * BRANCH_W)
CONV_COLS = (4 * BRANCH_W, 11 * BRANCH_W)
ATT_COLS = (11 * BRANCH_W, 13 * BRANCH_W + 2 * ATT_KV_W)
GATE_COLS = (13 * BRANCH_W + 2 * ATT_KV_W, 13 * BRANCH_W + 2 * ATT_KV_W + 4 * D_MODEL)

LANES = 128
SUBLANES = 8
CONV_HALO = 16
CONV_ROW_BLOCK = 64
TILE_X = 512
TILE_ATT = 1024
TILE_MERGE = 1024
MERGE_SUB_ROWS = 512
VMEM_LIMIT = 56 * 1024 * 1024

F32 = jnp.float32
BF16 = jnp.bfloat16


def _dot(a, b):
    return jnp.dot(a, b, preferred_element_type=F32)


def _dot_nt(a, b):
    return lax.dot_general(a, b, (((1,), (1,)), ((), ())), preferred_element_type=F32)


def _sigmoid(v):
    return jax.nn.sigmoid(v)


def _silu(v):
    return v * _sigmoid(v)


def _normed(x, mod_ref, nw_ref):
    a = nw_ref[...] * (1.0 + mod_ref[0, 1:2, :])
    shift = mod_ref[0, 0:1, :]
    ms = jnp.mean(x * x, axis=-1, keepdims=True)
    return (x * lax.rsqrt(ms + EPS) * a + shift).astype(BF16)


def _params():
    return pltpu.CompilerParams(vmem_limit_bytes=VMEM_LIMIT)


def _mod_kernel(c_ref, w_ref, b_ref, o_ref):
    cv = c_ref[...]
    o_ref[0] = _dot(_silu(cv).astype(BF16), w_ref[0].astype(BF16)) + b_ref[0]


def _modulation(cc, w_mod, b_mod):
    rows = cc.shape[0]
    tn = 1024
    return pl.pallas_call(
        _mod_kernel,
        grid=(DEPTH, 3 * D_MODEL // tn),
        in_specs=[pl.BlockSpec((rows, D_MODEL), lambda l, n: (0, 0)),
                  pl.BlockSpec((1, D_MODEL, tn), lambda l, n: (l, 0, n)),
                  pl.BlockSpec((1, 1, tn), lambda l, n: (l, 0, n))],
        out_specs=pl.BlockSpec((1, rows, tn), lambda l, n: (l, 0, n)),
        out_shape=jax.ShapeDtypeStruct((DEPTH, rows, 3 * D_MODEL), F32),
        compiler_params=_params(),
        name="modulation",
    )(cc, w_mod, b_mod.reshape(DEPTH, 1, 3 * D_MODEL))


_W_GROUPS = (RET_COLS, CONV_COLS, ATT_COLS, GATE_COLS)


def _split_kernel(w_ref, *outs):
    for (c0, c1), o in zip(_W_GROUPS, outs):
        o[0] = w_ref[0, :, c0:c1].astype(BF16)


def _split_projection(w_in):
    rows = 256
    return pl.pallas_call(
        _split_kernel,
        grid=(DEPTH, D_MODEL // rows),
        in_specs=[pl.BlockSpec((1, rows, w_in.shape[-1]), lambda l, r: (l, r, 0))],
        out_specs=[pl.BlockSpec((1, rows, c1 - c0), lambda l, r: (l, r, 0)) for c0, c1 in _W_GROUPS],
        out_shape=[jax.ShapeDtypeStruct((DEPTH, D_MODEL, c1 - c0), BF16) for c0, c1 in _W_GROUPS],
        compiler_params=_params(),
        name="split_projection",
    )(w_in)


_T_DEC, _T_QF, _T_KF, _T_QB, _T_KB, _T_CF, _T_CB = range(7)


def _ret_kernel(T, S, x_ref, mod_ref, nw_ref, w_ref, dec_ref, gn_ref, s0f_ref, s0b_ref,
                r_ref, sf_ref, sb_ref, qkv, zbuf, obuf, st_f, st_b, tab):
    t = pl.program_id(1)
    nT = S // T
    C = RET_CHUNK
    H = RET_HEADS
    k_scale = RET_DH ** -0.5

    @pl.when(t == 0)
    def _init():
        raw = dec_ref[...]
        lg = jnp.minimum(raw, 0.0) - jnp.log(1.0 + jnp.exp(-jnp.abs(raw)))
        ii = lax.broadcasted_iota(jnp.int32, (C, C), 0)
        jj = lax.broadcasted_iota(jnp.int32, (C, C), 1)
        diff = (ii - jj).astype(F32)
        pos = ii.astype(F32)
        for h in range(H):
            lf = lg[h:h + 1, :]
            lb = lg[H + h:H + h + 1, :]
            dec_f = jnp.where(diff >= 0, jnp.exp(jnp.where(diff >= 0, diff, 0.0) * lf), 0.0)
            dec_b = jnp.where(diff < 0, jnp.exp(jnp.where(diff < 0, -diff, 0.0) * lb), 0.0)
            tab[_T_DEC * H + h] = (dec_f + dec_b) * k_scale
            tab[_T_QF * H + h] = jnp.exp((pos + 1.0) * lf)
            tab[_T_KF * H + h] = jnp.exp((C - 1.0 - pos) * lf) * k_scale
            tab[_T_QB * H + h] = jnp.exp((C - pos) * lb)
            tab[_T_KB * H + h] = jnp.exp(pos * lb) * k_scale
            tab[_T_CF * H + h] = jnp.exp(float(C) * lf) + jnp.zeros((C, LANES), F32)
            tab[_T_CB * H + h] = jnp.exp(float(C) * lb) + jnp.zeros((C, LANES), F32)
        st_f[...] = s0f_ref[0]
        st_b[...] = s0b_ref[0]

    def pair(rows, part, hp):
        c0 = part * BRANCH_W + hp * RET_DH
        return qkv[rows, c0:c0 + 2 * RET_DH]

    def tab2(slot, hp):
        return jnp.concatenate([tab[slot * H + hp], tab[slot * H + hp + 1]], axis=1)

    def blockdiag(ab):
        n, m = ab.shape[0], ab.shape[1] // 2
        z = jnp.zeros((n, m), ab.dtype)
        return jnp.concatenate([jnp.concatenate([ab[:, :m], z], axis=1),
                                jnp.concatenate([z, ab[:, m:]], axis=1)], axis=0)

    def transpose2(ab):
        m = ab.shape[1] // 2
        return jnp.concatenate([jnp.transpose(ab[:, :m]), jnp.transpose(ab[:, m:])], axis=1)

    @pl.when(t < nT)
    def _forward():
        h_tile = _normed(x_ref[0], mod_ref, nw_ref)
        proj = _dot(h_tile, w_ref[0])
        row0 = pl.multiple_of(t * T, T)
        qkv[pl.ds(row0, T), :] = proj[:, :3 * BRANCH_W].astype(BF16)
        zbuf[pl.ds(row0, T), :] = proj[:, 3 * BRANCH_W:]
        for c in range(T // C):
            rows = pl.ds(pl.multiple_of(t * T + c * C, C), C)
            for hp in range(0, H, 2):
                q2, k2, v2 = pair(rows, 0, hp), pair(rows, 1, hp), pair(rows, 2, hp)
                sc2 = (_dot_nt(q2, blockdiag(k2)) * tab2(_T_DEC, hp)).astype(BF16)
                qs2 = (q2.astype(F32) * tab2(_T_QF, hp)).astype(BF16)
                kwt2 = transpose2(k2.astype(F32) * tab2(_T_KF, hp)).astype(BF16)
                res = _dot(jnp.concatenate([sc2, kwt2], axis=0), blockdiag(v2))
                cross = _dot(qs2, blockdiag(jnp.concatenate([st_f[hp], st_f[hp + 1]], axis=1).astype(BF16)))
                obuf[rows, hp * RET_DH:(hp + 2) * RET_DH] = res[:C] + cross
                for i in range(2):
                    st_f[hp + i] = st_f[hp + i] * tab[_T_CF * H + hp + i] + res[C:, i * RET_DH:(i + 1) * RET_DH]

        @pl.when(t == nT - 1)
        def _():
            sf_ref[0] = st_f[...]

    @pl.when(t == nT)
    def _backward():
        def chunk(i, carry):
            rows = pl.ds(pl.multiple_of((S // C - 1 - i) * C, C), C)
            for h in range(H):
                hc = slice(h * RET_DH, (h + 1) * RET_DH)
                q = qkv[rows, hc]
                k = qkv[rows, BRANCH_W + h * RET_DH:BRANCH_W + (h + 1) * RET_DH]
                v = qkv[rows, 2 * BRANCH_W + h * RET_DH:2 * BRANCH_W + (h + 1) * RET_DH]
                qs = (q.astype(F32) * tab[_T_QB * H + h]).astype(BF16)
                kwt = jnp.transpose(k.astype(F32) * tab[_T_KB * H + h]).astype(BF16)
                zero = jnp.zeros((C, RET_DH), BF16)
                lhs = jnp.concatenate([jnp.concatenate([qs, zero], axis=1),
                                       jnp.concatenate([zero, kwt], axis=1)], axis=0)
                res = _dot(lhs, jnp.concatenate([st_b[h].astype(BF16), v], axis=0))
                o = obuf[rows, hc] + res[:C]
                mu = jnp.mean(o, axis=-1, keepdims=True)
                d = o - mu
                var = jnp.mean(d * d, axis=-1, keepdims=True)
                y = d * lax.rsqrt(var + EPS) * gn_ref[:, hc]
                r_ref[0, rows, hc] = (y * _silu(zbuf[rows, hc])).astype(BF16)
                st_b[h] = st_b[h] * tab[_T_CB * H + h] + res[C:]
            return carry

        lax.fori_loop(0, S // C, chunk, 0, unroll=min(4, S // C))
        sb_ref[0] = st_b[...]


def _retention(xs, mod, nw, w_ret, l, dec, gn_w, s0f, s0b, T):
    B, S, _ = xs.shape
    nT = S // T
    st_shape = (RET_HEADS, RET_DH, RET_DH)
    wcols = RET_COLS[1] - RET_COLS[0]
    return pl.pallas_call(
        functools.partial(_ret_kernel, T, S),
        grid=(B, nT + 1),
        in_specs=[
            pl.BlockSpec((1, T, D_MODEL), lambda b, t: (b, jnp.minimum(t, nT - 1), 0)),
            pl.BlockSpec((1, 3, D_MODEL), lambda b, t: (b, 0, 0)),
            pl.BlockSpec((1, D_MODEL), lambda b, t: (0, 0)),
            pl.BlockSpec((1, D_MODEL, wcols), lambda b, t: (l, 0, 0), pipeline_mode=pl.Buffered(1)),
            pl.BlockSpec((2 * RET_HEADS, LANES), lambda b, t: (0, 0)),
            pl.BlockSpec((1, BRANCH_W), lambda b, t: (0, 0)),
            pl.BlockSpec((1,) + st_shape, lambda b, t: (b, 0, 0, 0)),
            pl.BlockSpec((1,) + st_shape, lambda b, t: (b, 0, 0, 0)),
        ],
        out_specs=[
            pl.BlockSpec((1, S, BRANCH_W), lambda b, t: (b, 0, 0)),
            pl.BlockSpec((1,) + st_shape, lambda b, t: (b, 0, 0, 0)),
            pl.BlockSpec((1,) + st_shape, lambda b, t: (b, 0, 0, 0)),
        ],
        out_shape=[
            jax.ShapeDtypeStruct((B, S, BRANCH_W), BF16),
            jax.ShapeDtypeStruct((B,) + st_shape, F32),
            jax.ShapeDtypeStruct((B,) + st_shape, F32),
        ],
        scratch_shapes=[
            pltpu.VMEM((S, 3 * BRANCH_W), BF16),
            pltpu.VMEM((S, BRANCH_W), F32),
            pltpu.VMEM((S, BRANCH_W), F32),
            pltpu.VMEM(st_shape, F32),
            pltpu.VMEM(st_shape, F32),
            pltpu.VMEM((7 * RET_HEADS, RET_CHUNK, LANES), F32),
        ],
        compiler_params=_params(),
        name="retention",
    )(xs, mod, nw, w_ret, dec, gn_w, s0f, s0b)


def _rope(val, rope_ref, r0, nrows):
    cos = rope_ref[0, pl.ds(r0, nrows), :]
    sin_lo = rope_ref[1, pl.ds(r0, nrows), :]
    sin_hi = rope_ref[2, pl.ds(r0, nrows), :]
    outs = []
    for g in range(val.shape[1] // LANES):
        vg = val[:, g * LANES:(g + 1) * LANES]
        outs.append(vg * cos + pltpu.roll(vg, LANES - ATT_DH // 2, 1) * sin_lo
                    + pltpu.roll(vg, ATT_DH // 2, 1) * sin_hi)
    return outs[0] if len(outs) == 1 else jnp.concatenate(outs, axis=1)


def _softmax_av(qh, parts, sink):
    scores = []
    m = None
    for k, _, mask in parts:
        s = _dot_nt(qh, k)
        if mask is not None:
            s = jnp.where(mask, s, NEG_INF)
        scores.append(s)
        sm = jnp.max(s, axis=-1, keepdims=True)
        m = sm if m is None else jnp.maximum(m, sm)
    m = jnp.maximum(m, sink)
    den = jnp.exp(sink - m)
    o = None
    for s, (_, v, _) in zip(scores, parts):
        pr = jnp.exp(s - m)
        den = den + jnp.sum(pr, axis=-1, keepdims=True)
        ov = _dot(pr.astype(BF16), v)
        o = ov if o is None else o + ov
    return o / den


def _v_ext_t(v, hk):
    lane = lax.broadcasted_iota(jnp.int32, v.shape, 1)
    vh = v if hk == 0 else pltpu.roll(v, ATT_DH, 1)
    return jnp.transpose(jnp.where(lane < ATT_DH, vh, 1.0)).astype(BF16)


def _att_local_kernel(T, S, x_ref, xp_ref, xn_ref, mod_ref, nw_ref, w_ref, rope_ref, kc_ref, vct_ref,
                      sink_ref, a_ref, q_s, k_s, vt_s, z_s, ot_s):
    t = pl.program_id(1)
    nb = T // ATT_BLOCK
    blk = ATT_BLOCK
    G = ATT_GROUP
    q_cols = slice(0, BRANCH_W)
    kv_cols = slice(BRANCH_W, BRANCH_W + 2 * ATT_KV_W)
    z_cols = slice(BRANCH_W + 2 * ATT_KV_W, 2 * BRANCH_W + 2 * ATT_KV_W)

    def put_kv(kv, r0, dst):
        k = _rope(kv[:, :ATT_KV_W], rope_ref, r0, kv.shape[0])
        for hk in range(ATT_KV_HEADS):
            k_s[hk, dst, :] = k[:, hk * ATT_DH:(hk + 1) * ATT_DH].astype(BF16)
            vt_s[hk, :, dst] = _v_ext_t(kv[:, ATT_KV_W:], hk)

    hm = _normed(x_ref[0], mod_ref, nw_ref)
    row0 = pl.multiple_of(t * T, T)
    q = _rope(_dot(hm, w_ref[0, :, q_cols]), rope_ref, row0, T) * (ATT_DH ** -0.5 * LOG2E)
    for j in range(nb):
        for hd in range(ATT_HEADS):
            hk, g = divmod(hd, G)
            q_s[j * ATT_KV_HEADS + hk, g * blk:(g + 1) * blk, :] = (
                q[j * blk:(j + 1) * blk, hd * ATT_DH:(hd + 1) * ATT_DH].astype(BF16))
    put_kv(_dot(hm, w_ref[0, :, kv_cols]), row0, slice(blk, blk + T))
    z_s[...] = _silu(_dot(hm, w_ref[0, :, z_cols]))

    rp = pl.multiple_of(jnp.maximum(t * T - blk, 0), blk)
    put_kv(_dot(_normed(xp_ref[0], mod_ref, nw_ref), w_ref[0, :, kv_cols]), rp, slice(0, blk))
    rn = pl.multiple_of(jnp.minimum((t + 1) * T, S - blk), blk)
    put_kv(_dot(_normed(xn_ref[0], mod_ref, nw_ref), w_ref[0, :, kv_cols]), rn, slice(blk + T, 2 * blk + T))

    NQ = G * blk
    kk = lax.broadcasted_iota(jnp.int32, (blk, NQ), 0)
    qq = lax.broadcasted_iota(jnp.int32, (blk, NQ), 1) & (blk - 1)
    grp = lax.broadcasted_iota(jnp.int32, (1, NQ), 1) // blk
    sinks = []
    for hk in range(ATT_KV_HEADS):
        sink = jnp.full((1, NQ), sink_ref[hk * G], F32)
        for g in range(1, G):
            sink = jnp.where(grp == g, sink_ref[hk * G + g], sink)
        sinks.append(sink * LOG2E)
    for j in range(nb):
        gb = t * nb + j
        wrows = slice(j * blk, (j + 3) * blk)
        qrows = slice(j * blk, (j + 1) * blk)
        for hk in range(ATT_KV_HEADS):
            q4 = q_s[j * ATT_KV_HEADS + hk]
            s_all = _dot_nt(jnp.concatenate([k_s[hk, wrows, :], kc_ref[0, hk]], axis=0), q4)
            s_prev = jnp.where((kk >= qq) & (gb > 0), s_all[:blk], NEG_INF)
            s_cur = s_all[blk:2 * blk]
            s_next = jnp.where((kk <= qq) & (gb < S // blk - 1), s_all[2 * blk:3 * blk], NEG_INF)
            s_ctx = s_all[3 * blk:]
            m = jnp.maximum(jnp.maximum(s_prev, s_cur), s_next)
            for cb in range(s_ctx.shape[0] // blk):
                m = jnp.maximum(m, s_ctx[cb * blk:(cb + 1) * blk])
            m = jnp.maximum(jnp.max(m, axis=0, keepdims=True), sinks[hk])
            p_all = jnp.concatenate([jnp.exp2(s_prev - m), jnp.exp2(s_cur - m), jnp.exp2(s_next - m),
                                     jnp.exp2(s_ctx - m)], axis=0).astype(BF16)
            o = _dot(jnp.concatenate([vt_s[hk, :, wrows], vct_ref[0, hk]], axis=1), p_all)
            res = o[:ATT_DH] / (o[ATT_DH:] + jnp.exp2(sinks[hk] - m))
            for g in range(G):
                hd = hk * G + g
                ot_s[j, hd * ATT_DH:(hd + 1) * ATT_DH, :] = res[:, g * blk:(g + 1) * blk]
        a_ref[0, qrows, :] = (jnp.transpose(ot_s[j]) * z_s[qrows, :]).astype(BF16)


def _att_ctx_kernel(x_ref, mod_ref, nw_ref, w_ref, sink_ref, a_ref, kc_ref, vct_ref, q_s, z_s):
    q_cols = slice(0, BRANCH_W)
    kv_cols = slice(BRANCH_W, BRANCH_W + 2 * ATT_KV_W)
    z_cols = slice(BRANCH_W + 2 * ATT_KV_W, 2 * BRANCH_W + 2 * ATT_KV_W)
    hm = _normed(x_ref[0], mod_ref, nw_ref)
    q_s[...] = (_dot(hm, w_ref[0, :, q_cols]) * (ATT_DH ** -0.5)).astype(BF16)
    kv = _dot(hm, w_ref[0, :, kv_cols])
    for hk in range(ATT_KV_HEADS):
        kc_ref[0, hk] = kv[:, hk * ATT_DH:(hk + 1) * ATT_DH].astype(BF16)
        vct_ref[0, hk] = _v_ext_t(kv[:, ATT_KV_W:], hk)
    z_s[...] = _silu(_dot(hm, w_ref[0, :, z_cols]))
    for hk in range(ATT_KV_HEADS):
        v_hk = kv[:, ATT_KV_W + hk * ATT_DH:ATT_KV_W + (hk + 1) * ATT_DH].astype(BF16)
        parts = [(kc_ref[0, hk], v_hk, None)]
        for g in range(ATT_GROUP):
            hd = hk * ATT_GROUP + g
            hc = slice(hd * ATT_DH, (hd + 1) * ATT_DH)
            o = _softmax_av(q_s[:, hc], parts, sink_ref[hd])
            a_ref[0, :, hc] = (o * z_s[:, hc]).astype(BF16)


def _attention_local(xs, mod, nw, w_att, l, rope_tab, kc, vc, sink, T):
    B, S, _ = xs.shape
    nT = S // T
    bpt = T // ATT_BLOCK
    nblk = S // ATT_BLOCK
    L = kc.shape[2]
    wcols = ATT_COLS[1] - ATT_COLS[0]
    return pl.pallas_call(
        functools.partial(_att_local_kernel, T, S),
        grid=(B, nT),
        in_specs=[
            pl.BlockSpec((1, T, D_MODEL), lambda b, t: (b, t, 0)),
            pl.BlockSpec((1, ATT_BLOCK, D_MODEL), lambda b, t: (b, jnp.maximum(t * bpt - 1, 0), 0)),
            pl.BlockSpec((1, ATT_BLOCK, D_MODEL), lambda b, t: (b, jnp.minimum((t + 1) * bpt, nblk - 1), 0)),
            pl.BlockSpec((1, 3, D_MODEL), lambda b, t: (b, 0, 0)),
            pl.BlockSpec((1, D_MODEL), lambda b, t: (0, 0)),
            pl.BlockSpec((1, D_MODEL, wcols), lambda b, t: (l, 0, 0)),
            pl.BlockSpec((3, S, LANES), lambda b, t: (0, 0, 0)),
            pl.BlockSpec((1, ATT_KV_HEADS, L, ATT_DH), lambda b, t: (b, 0, 0, 0)),
            pl.BlockSpec((1, ATT_KV_HEADS, 2 * ATT_DH, L), lambda b, t: (b, 0, 0, 0)),
            pl.BlockSpec(memory_space=pltpu.SMEM),
        ],
        out_specs=pl.BlockSpec((1, T, BRANCH_W), lambda b, t: (b, t, 0)),
        out_shape=jax.ShapeDtypeStruct((B, S, BRANCH_W), BF16),
        scratch_shapes=[
            pltpu.VMEM((bpt * ATT_KV_HEADS, ATT_GROUP * ATT_BLOCK, ATT_DH), BF16),
            pltpu.VMEM((ATT_KV_HEADS, T + 2 * ATT_BLOCK, ATT_DH), BF16),
            pltpu.VMEM((ATT_KV_HEADS, 2 * ATT_DH, T + 2 * ATT_BLOCK), BF16),
            pltpu.VMEM((T, BRANCH_W), F32),
            pltpu.VMEM((bpt, BRANCH_W, ATT_BLOCK), F32),
        ],
        compiler_params=_params(),
        name="attention_local",
    )(xs, xs, xs, mod, nw, w_att, rope_tab, kc, vc, sink)


def _attention_ctx(xc, mod, nw, w_att, l, sink):
    B, L, _ = xc.shape
    wcols = ATT_COLS[1] - ATT_COLS[0]
    return pl.pallas_call(
        _att_ctx_kernel,
        grid=(B,),
        in_specs=[
            pl.BlockSpec((1, L, D_MODEL), lambda b: (b, 0, 0)),
            pl.BlockSpec((1, 3, D_MODEL), lambda b: (b, 0, 0)),
            pl.BlockSpec((1, D_MODEL), lambda b: (0, 0)),
            pl.BlockSpec((1, D_MODEL, wcols), lambda b: (l, 0, 0)),
            pl.BlockSpec(memory_space=pltpu.SMEM),
        ],
        out_specs=[
            pl.BlockSpec((1, L, BRANCH_W), lambda b: (b, 0, 0)),
            pl.BlockSpec((1, ATT_KV_HEADS, L, ATT_DH), lambda b: (b, 0, 0, 0)),
            pl.BlockSpec((1, ATT_KV_HEADS, 2 * ATT_DH, L), lambda b: (b, 0, 0, 0)),
        ],
        out_shape=[
            jax.ShapeDtypeStruct((B, L, BRANCH_W), BF16),
            jax.ShapeDtypeStruct((B, ATT_KV_HEADS, L, ATT_DH), BF16),
            jax.ShapeDtypeStruct((B, ATT_KV_HEADS, 2 * ATT_DH, L), BF16),
        ],
        scratch_shapes=[
            pltpu.VMEM((L, BRANCH_W), BF16),
            pltpu.VMEM((L, BRANCH_W), F32),
        ],
        compiler_params=_params(),
        name="attention_ctx",
    )(xc, mod, nw, w_att, sink)


def _sc_phases():
    return tuple(sorted({(CONV_HALO - SC_KERNEL // 2 + k) % SUBLANES for k in range(SC_KERNEL)}))


def _conv_kernel(T, S, x_ref, xp_ref, xn_ref, mod_ref, nw_ref, w_ref, scw_ref, cfw_ref, cfb_ref,
                 lnw_ref, lnb_ref, s_ref, f_ref, hbuf, u_sc, u_cf, wrep):
    t = pl.program_id(1)
    nT = S // T
    HL = CONV_HALO
    W = BRANCH_W
    E = T + 2 * HL
    SUB = SUBLANES
    sc_phases = _sc_phases()
    hbuf[0:HL, :] = _normed(xp_ref[0], mod_ref, nw_ref)
    hbuf[HL:HL + T, :] = _normed(x_ref[0], mod_ref, nw_ref)
    hbuf[HL + T:2 * HL + T, :] = _normed(xn_ref[0], mod_ref, nw_ref)
    hext = hbuf[...]
    hm = hbuf[HL:HL + T, :]

    row = lax.broadcasted_iota(jnp.int32, (E, 1), 0)
    in_seq = ((row >= HL) | (t > 0)) & ((row < HL + T) | (t < nT - 1))

    def store_phases(u, dst, phases):
        for i, ph in enumerate(phases):
            dst[i, 0:E - SUB, :] = u[ph:ph + E - SUB, :]

    cx = _dot(hext, w_ref[0, :, W:3 * W])
    store_phases(jnp.where(in_seq, cx[:, :W] * cx[:, W:], 0.0), u_sc, sc_phases)
    glu = _dot(hext, w_ref[0, :, 4 * W:6 * W])
    store_phases(jnp.where(in_seq, glu[:, :W] * _sigmoid(glu[:, W:]), 0.0), u_cf, range(SUB))

    vec_rows = ([scw_ref[k:k + 1, :] for k in range(SC_KERNEL)] + [cfw_ref[k:k + 1, :] for k in range(CF_KERNEL)]
                + [cfb_ref[...], lnw_ref[...], lnb_ref[...]])
    for i, v in enumerate(vec_rows):
        wrep[i] = jnp.broadcast_to(v, (SUB, W))
    i_cfb, i_lnw, i_lnb = (SC_KERNEL + CF_KERNEL + i for i in range(3))

    RB = CONV_ROW_BLOCK

    def rows3(v):
        return v.reshape(RB // SUB, SUB, W)

    GR = 256
    for rb in range(T // RB):
        r = rb * RB
        if r % GR == 0:
            hg = hbuf[HL + r:HL + r + GR, :]
            b_gate = _dot(hg, w_ref[0, :, 0:W])
            z_sc = _dot(hg, w_ref[0, :, 3 * W:4 * W])
            z_cf = _dot(hg, w_ref[0, :, 6 * W:7 * W])
        rg = r % GR
        acc = None
        for k in range(SC_KERNEL):
            a, ph = divmod(HL - SC_KERNEL // 2 + k, SUB)
            term = wrep[k] * rows3(u_sc[sc_phases.index(ph), r + SUB * a:r + SUB * a + RB, :])
            acc = term if acc is None else acc + term
        acc = acc.reshape(RB, W)
        s_ref[0, r:r + RB, :] = (b_gate[rg:rg + RB] * acc * _silu(z_sc[rg:rg + RB])).astype(BF16)

        acc = wrep[i_cfb]
        for k in range(CF_KERNEL):
            a, ph = divmod(HL - CF_KERNEL // 2 + k, SUB)
            acc = acc + wrep[SC_KERNEL + k] * rows3(u_cf[ph, r + SUB * a:r + SUB * a + RB, :])
        acc = acc.reshape(RB, W)
        mu = jnp.mean(acc, axis=-1, keepdims=True)
        d = acc - mu
        var = jnp.mean(d * d, axis=-1, keepdims=True)
        y = (rows3(d * lax.rsqrt(var + EPS)) * wrep[i_lnw] + wrep[i_lnb]).reshape(RB, W)
        f_ref[0, r:r + RB, :] = (_silu(y) * _silu(z_cf[rg:rg + RB])).astype(BF16)


def _conv_branches(xs, mod, nw, w_conv, l, scw, cfw, cfb, lnw, lnb, T):
    B, S, _ = xs.shape
    nT = S // T
    HL = CONV_HALO
    bpt = T // HL
    nblk = S // HL
    wcols = CONV_COLS[1] - CONV_COLS[0]
    vec = lambda: pl.BlockSpec((1, BRANCH_W), lambda b, t: (0, 0))
    return pl.pallas_call(
        functools.partial(_conv_kernel, T, S),
        grid=(B, nT),
        in_specs=[
            pl.BlockSpec((1, T, D_MODEL), lambda b, t: (b, t, 0)),
            pl.BlockSpec((1, HL, D_MODEL), lambda b, t: (b, jnp.maximum(t * bpt - 1, 0), 0)),
            pl.BlockSpec((1, HL, D_MODEL), lambda b, t: (b, jnp.minimum((t + 1) * bpt, nblk - 1), 0)),
            pl.BlockSpec((1, 3, D_MODEL), lambda b, t: (b, 0, 0)),
            pl.BlockSpec((1, D_MODEL), lambda b, t: (0, 0)),
            pl.BlockSpec((1, D_MODEL, wcols), lambda b, t: (l, 0, 0)),
            pl.BlockSpec((SC_KERNEL, BRANCH_W), lambda b, t: (0, 0)),
            pl.BlockSpec((CF_KERNEL, BRANCH_W), lambda b, t: (0, 0)),
            vec(), vec(), vec(),
        ],
        out_specs=[pl.BlockSpec((1, T, BRANCH_W), lambda b, t: (b, t, 0)),
                   pl.BlockSpec((1, T, BRANCH_W), lambda b, t: (b, t, 0))],
        out_shape=[jax.ShapeDtypeStruct((B, S, BRANCH_W), BF16),
                   jax.ShapeDtypeStruct((B, S, BRANCH_W), BF16)],
        scratch_shapes=[
            pltpu.VMEM((T + 2 * HL, D_MODEL), BF16),
            pltpu.VMEM((len(_sc_phases()), T + 2 * HL, BRANCH_W), F32),
            pltpu.VMEM((SUBLANES, T + 2 * HL, BRANCH_W), F32),
            pltpu.VMEM((SC_KERNEL + CF_KERNEL + 3, SUBLANES, BRANCH_W), F32),
        ],
        compiler_params=_params(),
        name="conv_branches",
    )(xs, xs, xs, mod, nw, w_conv, scw, cfw, cfb, lnw, lnb)


def _merge_kernel(final, x_ref, mod_ref, nw_ref, wg_ref, r_ref, s_ref, f_ref, a_ref, wb_ref, wo_ref,
                  fnw_ref, o_ref):
    T = x_ref.shape[1]
    sub = min(T, MERGE_SUB_ROWS)
    for r0 in range(0, T, sub):
        rows = slice(r0, r0 + sub)
        x = x_ref[0, rows, :]
        h = _normed(x, mod_ref, nw_ref)
        merged = None
        for i, br in enumerate((r_ref, s_ref, f_ref, a_ref)):
            g = _dot(h, wg_ref[0, :, i * D_MODEL:(i + 1) * D_MODEL])
            y = _sigmoid(g) * _dot(br[0, rows, :], wb_ref[0, i])
            merged = y if merged is None else merged + y
        out = _dot(merged.astype(BF16), wo_ref[0])
        xn = x + mod_ref[0, 2:3, :] * out
        if final:
            ms = jnp.mean(xn * xn, axis=-1, keepdims=True)
            xn = xn * lax.rsqrt(ms + EPS) * fnw_ref[...]
        o_ref[0, rows, :] = xn


def _merge(xs, mod, nw, w_gate, l, r, s, f, a, w_branch, w_out, fnw, T, final):
    B, S, _ = xs.shape
    nT = S // T
    br = lambda: pl.BlockSpec((1, T, BRANCH_W), lambda b, t: (b, t, 0))
    return pl.pallas_call(
        functools.partial(_merge_kernel, final),
        grid=(B, nT),
        in_specs=[
            pl.BlockSpec((1, T, D_MODEL), lambda b, t: (b, t, 0)),
            pl.BlockSpec((1, 3, D_MODEL), lambda b, t: (b, 0, 0)),
            pl.BlockSpec((1, D_MODEL), lambda b, t: (0, 0)),
            pl.BlockSpec((1, D_MODEL, 4 * D_MODEL), lambda b, t: (l, 0, 0), pipeline_mode=pl.Buffered(1)),
            br(), br(), br(), br(),
            pl.BlockSpec((1, 4, BRANCH_W, D_MODEL), lambda b, t: (l, 0, 0, 0), pipeline_mode=pl.Buffered(1)),
            pl.BlockSpec((1, D_MODEL, D_MODEL), lambda b, t: (l, 0, 0), pipeline_mode=pl.Buffered(1)),
            pl.BlockSpec((1, D_MODEL), lambda b, t: (0, 0)),
        ],
        out_specs=pl.BlockSpec((1, T, D_MODEL), lambda b, t: (b, t, 0)),
        out_shape=jax.ShapeDtypeStruct((B, S, D_MODEL), F32),
        compiler_params=_params(),
        name="merge",
    )(xs, mod, nw, w_gate, r, s, f, a, w_branch, w_out, fnw)


def _rope_table(S):
    rows = S // GRID_W
    row = jnp.repeat(jnp.arange(rows), GRID_W).astype(F32)
    col = jnp.tile(jnp.arange(GRID_W), rows).astype(F32)
    inv = ROPE_BASE ** (-jnp.arange(ROPE_AXIS_FREQS, dtype=F32) / ROPE_AXIS_FREQS)
    ang = jnp.concatenate([row[:, None] * inv[None], col[:, None] * inv[None]], axis=-1)
    cos, sin = jnp.cos(ang), jnp.sin(ang)
    zero = jnp.zeros_like(sin)
    reps = LANES // ATT_DH
    cos_t = jnp.tile(jnp.concatenate([cos, cos], axis=-1), (1, reps))
    sin_lo = jnp.tile(jnp.concatenate([-sin, zero], axis=-1), (1, reps))
    sin_hi = jnp.tile(jnp.concatenate([zero, sin], axis=-1), (1, reps))
    return jnp.stack([cos_t, sin_lo, sin_hi])


def kernel(x, c, ctx, c_ctx, w_mod, b_mod, norm_w, w_in, ret_decay, ret_gn_w, sc_conv_w, cf_conv_w,
           cf_conv_b, cf_ln_w, cf_ln_b, att_sink, w_branch, w_out, final_norm_w):
    B, S, D = x.shape
    L = ctx.shape[1]
    T = TILE_X

    w_ret, w_conv, w_att, w_gate = _split_projection(w_in)
    w_br = w_branch.astype(BF16)
    w_o = w_out.astype(BF16)
    rope_tab = _rope_table(S)

    mod_rows = 16
    cc = jnp.concatenate([c, c_ctx[None], jnp.zeros((mod_rows - B - 1, D), F32)], axis=0)
    mod_all = _modulation(cc, w_mod, b_mod)
    mod_x = mod_all[:, :B].reshape(DEPTH, B, 3, D)
    mod_c = jnp.broadcast_to(mod_all[:, B:B + 1].reshape(DEPTH, 1, 3, D), (DEPTH, B, 3, D))

    zero_state = jnp.zeros((B, RET_HEADS, RET_DH, RET_DH), F32)
    xc = ctx
    for l in range(DEPTH):
        last = l == DEPTH - 1
        nw = norm_w[l][None]
        dec = jnp.broadcast_to(ret_decay[l].reshape(2 * RET_HEADS, 1), (2 * RET_HEADS, LANES))
        gn = ret_gn_w[l][None]
        vecs = (sc_conv_w[l], cf_conv_w[l], cf_conv_b[l][None], cf_ln_w[l][None], cf_ln_b[l][None])
        fnw = final_norm_w[None]

        r_c, s_f, s_b = _retention(xc, mod_c[l], nw, w_ret, l, dec, gn, zero_state, zero_state, L)
        a_c, k_c, v_c = _attention_ctx(xc, mod_c[l], nw, w_att, l, att_sink[l])

        r_x, _, _ = _retention(x, mod_x[l], nw, w_ret, l, dec, gn, s_f, s_b, T)
        a_x = _attention_local(x, mod_x[l], nw, w_att, l, rope_tab, k_c, v_c, att_sink[l], TILE_ATT)
        s_x, f_x = _conv_branches(x, mod_x[l], nw, w_conv, l, *vecs, T)
        x = _merge(x, mod_x[l], nw, w_gate, l, r_x, s_x, f_x, a_x, w_br, w_o, fnw, TILE_MERGE, last)

        if not last:
            s_c, f_c = _conv_branches(xc, mod_c[l], nw, w_conv, l, *vecs, L)
            xc = _merge(xc, mod_c[l], nw, w_gate, l, r_c, s_c, f_c, a_c, w_br, w_o, fnw, L, False)
    return x
```
